```python
import jax, jax.numpy as jnp
from jax import lax
import numpy as np

D_MODEL = 1024
BATCH = 8
SEQ = 4096
DEPTH = 1
DEC_BATCH = 16
DEC_SEQ = 16
PAST_LEN = 2048

CHUNK = 64
D_RNN = D_MODEL
N_LRU_BLOCKS = 16
LRU_BLOCK = D_RNN // N_LRU_BLOCKS
LRU_CONV = 4
LRU_C = 8.0
D_POOL = D_MODEL
POOL_WINDOWS = (2, 4, 8, 16)
N_POOL_GROUPS = len(POOL_WINDOWS)
POOL_GROUP = D_POOL // N_POOL_GROUPS
POOL_HIST = max(POOL_WINDOWS) - 1
N_BRANCH = 2
D_IN = D_RNN + D_POOL + N_BRANCH * D_MODEL
D_FF = 3 * D_MODEL
FFN_CONV = 3
EPS = 1e-6

kernel_name = 'griffin_pool_hybrid_stream_step'


def rmsnorm(x, g):
    xf = x.astype(jnp.float32)
    y = xf * lax.rsqrt(jnp.mean(xf * xf, axis=-1, keepdims=True) + EPS)
    return (y * g.astype(jnp.float32)).astype(x.dtype)


def causal_dwconv(x, prev, w, b):
    width = w.shape[0]
    t_len = x.shape[1]
    buf = jnp.concatenate([prev.astype(x.dtype), x], axis=1)
    y = buf[:, 0:t_len] * w[0]
    for k in range(1, width):
        y = y + buf[:, k:k + t_len] * w[k]
    return y + b, buf[:, -(width - 1):]


def rg_lru(u, h0, w_a, b_a, w_x, b_x, lam):
    bsz, t_len, _ = u.shape
    ub = u.reshape(bsz, t_len, N_LRU_BLOCKS, LRU_BLOCK)
    r = jax.nn.sigmoid(jnp.einsum('btnc,ncd->btnd', ub, w_a).reshape(bsz, t_len, D_RNN) + b_a)
    i = jax.nn.sigmoid(jnp.einsum('btnc,ncd->btnd', ub, w_x).reshape(bsz, t_len, D_RNN) + b_x)
    log_a = (-LRU_C * r.astype(jnp.float32)) * jax.nn.softplus(-lam.astype(jnp.float32))
    a = jnp.exp(log_a)
    mult = jnp.sqrt(-jnp.expm1(2.0 * log_a))
    b = mult * (i * u).astype(jnp.float32)
    b = b.at[:, 0].add(a[:, 0] * h0.astype(jnp.float32))

    def combine(left, right):
        a1, b1 = left
        a2, b2 = right
        return a1 * a2, a2 * b1 + b2

    _, h = lax.associative_scan(combine, (a, b), axis=1)
    return h.astype(u.dtype), h[:, -1]


def multiscale_pool(p, prev, offset, w_pool, scale):
    bsz, t_len, _ = p.shape
    buf = jnp.concatenate([prev.astype(p.dtype), p], axis=1)
    buf_f = buf.astype(jnp.float32)
    cs = jnp.concatenate([jnp.zeros((bsz, 1, D_POOL), jnp.float32), jnp.cumsum(buf_f, axis=1)], axis=1)
    end = cs[:, POOL_HIST + 1:]
    pos = offset + jnp.arange(t_len)
    outs = []
    for g, w in enumerate(POOL_WINDOWS):
        sl = slice(g * POOL_GROUP, (g + 1) * POOL_GROUP)
        start = POOL_HIST + 1 - w
        s = end[..., sl] - cs[:, start:start + t_len, sl]
        cnt = jnp.minimum(pos + 1, w).astype(jnp.float32)[None, :, None]
        outs.append(s / cnt)
    pooled = (jnp.concatenate(outs, axis=-1) - p.astype(jnp.float32)).astype(p.dtype)
    pooled = pooled.reshape(bsz, t_len, N_POOL_GROUPS, POOL_GROUP)
    y = jnp.einsum('btgc,gcd->btgd', pooled, w_pool).reshape(bsz, t_len, D_POOL) * scale
    return y, buf[:, -POOL_HIST:]


def conv_ffn(xn, prev, w_up, w_conv, b_conv, w_down):
    h = xn @ w_up
    h, new_buf = causal_dwconv(h, prev, w_conv, b_conv)
    g, v = h[..., :D_FF], h[..., D_FF:]
    return (jax.nn.gelu(g) * v) @ w_down, new_buf


def trunk_layer(x, h0, lru_buf, pool_buf, ffn_buf, offset,
                norm_mix, w_in, conv_lru_w, conv_lru_b, w_ra, b_ra, w_ix, b_ix, lru_lambda,
                w_pool, pool_scale, w_br_lru, w_br_pool, w_out,
                norm_ffn, w_up, conv_ffn_w, conv_ffn_b, w_down):
    bsz, t_len, _ = x.shape
    xn = rmsnorm(x, norm_mix)
    z = xn @ w_in
    x_rnn = z[..., :D_RNN]
    x_pool = z[..., D_RNN:D_RNN + D_POOL]
    gate_logits = z[..., D_RNN + D_POOL:].reshape(bsz, t_len, N_BRANCH, D_MODEL)
    u, new_lru_buf = causal_dwconv(x_rnn, lru_buf, conv_lru_w, conv_lru_b)
    h, h_last = rg_lru(u, h0, w_ra, b_ra, w_ix, b_ix, lru_lambda)
    pp, new_pool_buf = multiscale_pool(x_pool, pool_buf, offset, w_pool, pool_scale)
    gates = jax.nn.sigmoid(gate_logits)
    merged = gates[:, :, 0] * (h @ w_br_lru) + gates[:, :, 1] * (pp @ w_br_pool)
    x = x + merged @ w_out
    f, new_ffn_buf = conv_ffn(rmsnorm(x, norm_ffn), ffn_buf, w_up, conv_ffn_w, conv_ffn_b, w_down)
    x = x + f
    return x, h_last, new_lru_buf, new_pool_buf, new_ffn_buf


def run_stack(x, st_h, st_lru, st_pool, st_ffn, offset, layer_w, norm_final):
    hs, lrus, pools, ffns = [], [], [], []
    for l in range(DEPTH):
        x, h_last, lb, pb, fb = trunk_layer(x, st_h[l], st_lru[l], st_pool[l], st_ffn[l], offset,
                                           *[w[l] for w in layer_w])
        hs.append(h_last)
        lrus.append(lb)
        pools.append(pb)
        ffns.append(fb)
    return (rmsnorm(x, norm_final), jnp.stack(hs), jnp.stack(lrus), jnp.stack(pools), jnp.stack(ffns))


def setup_inputs(seed: int = 0) -> dict:
    key = jax.random.key(seed)
    ks = jax.random.split(key, 32)

    def nrm(k, shape, scale):
        return jax.random.normal(k, shape, jnp.float32) * scale

    u_lam = jax.random.uniform(ks[12], (DEPTH, D_RNN), jnp.float32, 0.9, 0.999)
    return {
        'x_prompt': nrm(ks[0], (BATCH, SEQ, D_MODEL), 1.0),
        'x_sample': nrm(ks[1], (DEC_BATCH, DEC_SEQ, D_MODEL), 1.0),
        'state_lru_h': nrm(ks[2], (DEPTH, DEC_BATCH, D_RNN), 0.5),
        'state_lru_conv': nrm(ks[3], (DEPTH, DEC_BATCH, LRU_CONV - 1, D_RNN), 1.0),
        'state_pool': nrm(ks[4], (DEPTH, DEC_BATCH, POOL_HIST, D_POOL), 1.0),
        'state_ffn_conv': nrm(ks[5], (DEPTH, DEC_BATCH, FFN_CONV - 1, 2 * D_FF), 1.0),
        'norm_mix': 1.0 + nrm(ks[6], (DEPTH, D_MODEL), 0.02),
        'w_in': nrm(ks[7], (DEPTH, D_MODEL, D_IN), D_MODEL ** -0.5),
        'conv_lru_w': nrm(ks[8], (DEPTH, LRU_CONV, D_RNN), 0.5),
        'conv_lru_b': nrm(ks[9], (DEPTH, D_RNN), 0.02),
        'w_ra': nrm(ks[10], (DEPTH, N_LRU_BLOCKS, LRU_BLOCK, LRU_BLOCK), LRU_BLOCK ** -0.5),
        'b_ra': nrm(ks[11], (DEPTH, D_RNN), 0.02),
        'w_ix': nrm(ks[13], (DEPTH, N_LRU_BLOCKS, LRU_BLOCK, LRU_BLOCK), LRU_BLOCK ** -0.5),
        'b_ix': nrm(ks[14], (DEPTH, D_RNN), 0.02),
        'lru_lambda': jnp.log(u_lam) - jnp.log1p(-u_lam),
        'w_pool': nrm(ks[15], (DEPTH, N_POOL_GROUPS, POOL_GROUP, POOL_GROUP), POOL_GROUP ** -0.5),
        'pool_scale': 1.0 + nrm(ks[16], (DEPTH, D_POOL), 0.02),
        'w_br_lru': nrm(ks[17], (DEPTH, D_RNN, D_MODEL), D_RNN ** -0.5),
        'w_br_pool': nrm(ks[18], (DEPTH, D_POOL, D_MODEL), D_POOL ** -0.5),
        'w_out': nrm(ks[19], (DEPTH, D_MODEL, D_MODEL), D_MODEL ** -0.5),
        'norm_ffn': 1.0 + nrm(ks[20], (DEPTH, D_MODEL), 0.02),
        'w_up': nrm(ks[21], (DEPTH, D_MODEL, 2 * D_FF), D_MODEL ** -0.5),
        'conv_ffn_w': nrm(ks[22], (DEPTH, FFN_CONV, 2 * D_FF), 0.5),
        'conv_ffn_b': nrm(ks[23], (DEPTH, 2 * D_FF), 0.02),
        'w_down': nrm(ks[24], (DEPTH, D_FF, D_MODEL), D_FF ** -0.5),
        'norm_final': 1.0 + nrm(ks[25], (D_MODEL,), 0.02),
    }


def reference(x_prompt, x_sample, state_lru_h, state_lru_conv, state_pool, state_ffn_conv,
              norm_mix, w_in, conv_lru_w, conv_lru_b, w_ra, b_ra, w_ix, b_ix, lru_lambda,
              w_pool, pool_scale, w_br_lru, w_br_pool, w_out,
              norm_ffn, w_up, conv_ffn_w, conv_ffn_b, w_down, norm_final):
    layer_w = (norm_mix, w_in, conv_lru_w, conv_lru_b, w_ra, b_ra, w_ix, b_ix, lru_lambda,
               w_pool, pool_scale, w_br_lru, w_br_pool, w_out,
               norm_ffn, w_up, conv_ffn_w, conv_ffn_b, w_down)
    dt = x_prompt.dtype
    p_h0 = jnp.zeros((DEPTH, BATCH, D_RNN), jnp.float32)
    p_lru0 = jnp.zeros((DEPTH, BATCH, LRU_CONV - 1, D_RNN), dt)
    p_pool0 = jnp.zeros((DEPTH, BATCH, POOL_HIST, D_POOL), dt)
    p_ffn0 = jnp.zeros((DEPTH, BATCH, FFN_CONV - 1, 2 * D_FF), dt)
    y_prompt, p_h, p_lru, p_pool, p_ffn = run_stack(x_prompt, p_h0, p_lru0, p_pool0, p_ffn0, 0,
                                                    layer_w, norm_final)
    y_sample, s_h, s_lru, s_pool, s_ffn = run_stack(x_sample, state_lru_h, state_lru_conv, state_pool,
                                                    state_ffn_conv, PAST_LEN, layer_w, norm_final)
    return (y_prompt, y_sample, p_h, p_lru, p_pool, p_ffn, s_h, s_lru, s_pool, s_ffn)
```

```python
import functools
import math

import jax
import jax.numpy as jnp
from jax import lax
from jax.experimental import pallas as pl
from jax.experimental.pallas import tpu as pltpu

D_MODEL = 1024
D_RNN = D_MODEL
D_POOL = D_MODEL
D_FF = 3 * D_MODEL
N_LRU_BLOCKS = 16
LRU_BLOCK = D_RNN // N_LRU_BLOCKS
LRU_CONV = 4
LRU_C = 8.0
POOL_WINDOWS = (2, 4, 8, 16)
POOL_GROUP = D_POOL // len(POOL_WINDOWS)
POOL_HIST = max(POOL_WINDOWS) - 1
FFN_CONV = 3
EPS = 1e-6
PAST_LEN = 2048

MXU_DIM_V7X = 256
GATE_CHUNKS = D_RNN // MXU_DIM_V7X
VMEM_LIMIT_BYTES = 58 * 1024 * 1024
ROW_CHUNK = 64
FFN_COL_CHUNK = 512
GELU_C = math.sqrt(2.0 / math.pi)

_BF16 = jnp.bfloat16
_F32 = jnp.float32


def _sigmoid(x):
    return 0.5 * jnp.tanh(0.5 * x) + 0.5


def _dot(a, b):
    return jnp.dot(a, b, preferred_element_type=_F32)


def _row_loop(n_rows, chunk, body):
    def step(i, carry):
        body(pl.multiple_of(i * chunk, chunk))
        return carry
    lax.fori_loop(0, n_rows // chunk, step, 0)


def _layer_kernel(
        x_ref, h0_ref, lru0_ref, pool0_ref, ffn0_ref,
        norm_mix_ref, w_in_ref, conv_lru_w_ref, conv_lru_b_ref, w_gate_ref, b_ra_ref, b_ix_ref,
        lam_ref, w_pool_ref, pool_scale_ref, w_br_lru_ref, w_br_pool_ref, w_out_ref,
        norm_ffn_ref, w_up_ref, conv_ffn_w_ref, conv_ffn_b_ref, w_down_ref, norm_final_ref,
        y_ref, h_ref, lru_new_ref, pool_new_ref, ffn_new_ref,
        rnn_buf, pool_buf, gate_buf, hup_buf, xn_buf, act_buf,
        *, nb, tt, offset):
    tm = nb * tt
    lru_hist = (LRU_CONV - 1) * nb
    pool_hist = POOL_HIST * nb
    ffn_hist = (FFN_CONV - 1) * nb
    step = pl.program_id(0)
    rows = ROW_CHUNK

    @pl.when(step == 0)
    def _():
        h_ref[...] = h0_ref[...]
        rnn_buf[0:lru_hist, :] = lru0_ref[...]
        pool_buf[0:pool_hist, :] = pool0_ref[...]
        hup_buf[0:ffn_hist, :] = ffn0_ref[...]

    def tmp(k):
        return hup_buf.at[ffn_hist:ffn_hist + tm, k * D_MODEL:(k + 1) * D_MODEL]
    u_tmp, r_tmp, i_tmp, a_tmp, b_tmp, h_tmp = (tmp(k) for k in range(6))
    ya_tmp, yb_tmp = u_tmp, r_tmp

    g_mix = norm_mix_ref[...]

    def norm_mix_body(r0):
        x = x_ref[pl.ds(r0, rows), :]
        ms = jnp.mean(x * x, axis=-1, keepdims=True)
        xn_buf[pl.ds(r0, rows), :] = ((x * lax.rsqrt(ms + EPS)) * g_mix).astype(_BF16)
    _row_loop(tm, rows, norm_mix_body)

    xn = xn_buf[...]
    rnn_buf[lru_hist:, :] = _dot(xn, w_in_ref[:, 0:D_RNN])
    pool_buf[pool_hist:, :] = _dot(xn, w_in_ref[:, D_RNN:D_RNN + D_POOL])
    gate_buf[...] = _dot(xn, w_in_ref[:, D_RNN + D_POOL:])

    def lru_conv_body(r0):
        u = rnn_buf[pl.ds(r0, rows), :] * conv_lru_w_ref[0:1, :]
        for k in range(1, LRU_CONV):
            u = u + rnn_buf[pl.ds(r0 + k * nb, rows), :] * conv_lru_w_ref[k:k + 1, :]
        u = u + conv_lru_b_ref[...]
        u_tmp[pl.ds(r0, rows), :] = u
        xn_buf[pl.ds(r0, rows), :] = u.astype(_BF16)
    _row_loop(tm, rows, lru_conv_body)

    for c in range(GATE_CHUNKS):
        cols = slice(c * MXU_DIM_V7X, (c + 1) * MXU_DIM_V7X)
        ri = _dot(xn_buf[:, cols], w_gate_ref[c])
        r_tmp[:, cols] = ri[:, :MXU_DIM_V7X]
        i_tmp[:, cols] = ri[:, MXU_DIM_V7X:]

    lam = lam_ref[...]
    softplus_neg_lam = jnp.maximum(-lam, 0.0) + jnp.log1p(jnp.exp(-jnp.abs(lam)))
    half_log_a_scale = (-0.5 * LRU_C) * softplus_neg_lam

    def lru_coef_body(r0):
        sl = pl.ds(r0, rows)
        u = u_tmp[sl, :]
        r = _sigmoid(r_tmp[sl, :] + b_ra_ref[...])
        i = _sigmoid(i_tmp[sl, :] + b_ix_ref[...])
        s = jnp.tanh(r * half_log_a_scale)
        q = 1.0 / (1.0 - s)
        a_tmp[sl, :] = (1.0 + s) * q
        b_tmp[sl, :] = ((2.0 * jnp.sqrt(-s)) * q) * (i * u)
    _row_loop(tm, rows, lru_coef_body)

    def scan_body(t, h):
        sl = pl.ds(pl.multiple_of(t * nb, nb), nb)
        h = a_tmp[sl, :] * h + b_tmp[sl, :]
        h_tmp[sl, :] = h
        return h
    h_ref[...] = lax.fori_loop(0, tt, scan_body, h_ref[...], unroll=4)

    ya_tmp[...] = _dot(h_tmp[...].astype(_BF16), w_br_lru_ref[...])

    row_t = lax.broadcasted_iota(jnp.int32, (tm, 128), 0) // nb
    pos1 = row_t + (offset + 1) + step * tt
    for g, w in enumerate(POOL_WINDOWS):
        cols = slice(g * POOL_GROUP, (g + 1) * POOL_GROUP)
        first = POOL_HIST + 1 - w
        s = pool_buf[first * nb:(first + w - 1) * nb + tm, cols]
        span = 1
        while span < w:
            s = s[span * nb:, :] + s[:s.shape[0] - span * nb, :]
            span *= 2
        cnt = jnp.minimum(pos1, w).astype(_F32)
        cnt = jnp.concatenate([cnt] * (POOL_GROUP // 128), axis=1)
        pooled = s / cnt - pool_buf[pool_hist:, cols]
        pp = _dot(pooled.astype(_BF16), w_pool_ref[g]) * pool_scale_ref[:, cols]
        xn_buf[:, cols] = pp.astype(_BF16)
    yb_tmp[...] = _dot(xn_buf[...], w_br_pool_ref[...])

    def merge_body(r0):
        sl = pl.ds(r0, rows)
        m = (_sigmoid(gate_buf[sl, 0:D_MODEL]) * ya_tmp[sl, :]
             + _sigmoid(gate_buf[sl, D_MODEL:]) * yb_tmp[sl, :])
        xn_buf[sl, :] = m.astype(_BF16)
    _row_loop(tm, rows, merge_body)
    y_ref[...] = x_ref[...] + _dot(xn_buf[...], w_out_ref[...])

    g_ffn = norm_ffn_ref[...]

    def norm_ffn_body(r0):
        x = y_ref[pl.ds(r0, rows), :]
        ms = jnp.mean(x * x, axis=-1, keepdims=True)
        xn_buf[pl.ds(r0, rows), :] = ((x * lax.rsqrt(ms + EPS)) * g_ffn).astype(_BF16)
    _row_loop(tm, rows, norm_ffn_body)

    xn = xn_buf[...]
    for c in range(3):
        cols = slice(c * 2 * D_MODEL, (c + 1) * 2 * D_MODEL)
        hup_buf[ffn_hist:, cols] = _dot(xn, w_up_ref[:, cols])

    def conv_cols(r0, cols):
        y = hup_buf[pl.ds(r0, rows), cols] * conv_ffn_w_ref[0:1, cols]
        for k in range(1, FFN_CONV):
            y = y + hup_buf[pl.ds(r0 + k * nb, rows), cols] * conv_ffn_w_ref[k:k + 1, cols]
        return y + conv_ffn_b_ref[:, cols]

    def geglu_body(r0):
        for c in range(D_FF // FFN_COL_CHUNK):
            gcols = slice(c * FFN_COL_CHUNK, (c + 1) * FFN_COL_CHUNK)
            vcols = slice(D_FF + c * FFN_COL_CHUNK, D_FF + (c + 1) * FFN_COL_CHUNK)
            gt = conv_cols(r0, gcols)
            v = conv_cols(r0, vcols)
            cdf = 0.5 * (1.0 + jnp.tanh(GELU_C * (gt + 0.044715 * (gt * gt * gt))))
            act_buf[pl.ds(r0, rows), gcols] = ((gt * cdf) * v).astype(_BF16)
    _row_loop(tm, rows, geglu_body)

    y_ref[...] = y_ref[...] + _dot(act_buf[...], w_down_ref[...])

    g_final = norm_final_ref[...]

    def norm_final_body(r0):
        x = y_ref[pl.ds(r0, rows), :]
        ms = jnp.mean(x * x, axis=-1, keepdims=True)
        y_ref[pl.ds(r0, rows), :] = (x * lax.rsqrt(ms + EPS)) * g_final
    _row_loop(tm, rows, norm_final_body)

    rnn_buf[0:lru_hist, :] = rnn_buf[tm:tm + lru_hist, :]
    pool_buf[0:pool_hist, :] = pool_buf[tm:tm + pool_hist, :]
    hup_buf[0:ffn_hist, :] = hup_buf[tm:tm + ffn_hist, :]

    @pl.when(step == pl.num_programs(0) - 1)
    def _():
        lru_new_ref[...] = rnn_buf[0:lru_hist, :]
        pool_new_ref[...] = pool_buf[0:pool_hist, :]
        ffn_new_ref[...] = hup_buf[0:ffn_hist, :]


def _full_spec(a):
    zeros = (0,) * a.ndim
    return pl.BlockSpec(a.shape, lambda i: zeros)


def _run_stack(x_tb, h0, lru0, pool0, ffn0, weights, *, nb, tt, offset, name):
    n_rows = x_tb.shape[0]
    tm = nb * tt
    assert n_rows % tm == 0 and tm % ROW_CHUNK == 0 and tt > POOL_HIST
    small = (h0, lru0, pool0, ffn0)
    out_shape = (
        jax.ShapeDtypeStruct((n_rows, D_MODEL), _F32),
        jax.ShapeDtypeStruct(h0.shape, _F32),
        jax.ShapeDtypeStruct(lru0.shape, _F32),
        jax.ShapeDtypeStruct(pool0.shape, _F32),
        jax.ShapeDtypeStruct(ffn0.shape, _F32),
    )
    row_spec = pl.BlockSpec((tm, D_MODEL), lambda i: (i, 0))
    return pl.pallas_call(
        functools.partial(_layer_kernel, nb=nb, tt=tt, offset=offset),
        grid=(n_rows // tm,),
        in_specs=[row_spec] + [_full_spec(a) for a in small + tuple(weights)],
        out_specs=(row_spec,) + tuple(_full_spec(a) for a in small),
        out_shape=out_shape,
        scratch_shapes=[
            pltpu.VMEM(((LRU_CONV - 1) * nb + tm, D_RNN), _F32),
            pltpu.VMEM((POOL_HIST * nb + tm, D_POOL), _F32),
            pltpu.VMEM((tm, 2 * D_MODEL), _F32),
            pltpu.VMEM(((FFN_CONV - 1) * nb + tm, 2 * D_FF), _F32),
            pltpu.VMEM((tm, D_MODEL), _BF16),
            pltpu.VMEM((tm, D_FF), _BF16),
        ],
        compiler_params=pltpu.CompilerParams(
            dimension_semantics=("arbitrary",), vmem_limit_bytes=VMEM_LIMIT_BYTES),
        name=name,
    )(x_tb, *small, *weights)


def _block_diag_tiles(w):
    per = MXU_DIM_V7X // LRU_BLOCK
    w4 = w.reshape(GATE_CHUNKS, per, LRU_BLOCK, LRU_BLOCK)
    eye = jnp.eye(per, dtype=w.dtype)
    return jnp.einsum('ciab,ij->ciajb', w4, eye).reshape(GATE_CHUNKS, MXU_DIM_V7X, MXU_DIM_V7X)


def _to_time_major(a):
    return jnp.swapaxes(a, 0, 1).reshape(a.shape[0] * a.shape[1], a.shape[2])


def _from_time_major(a, nb):
    return jnp.swapaxes(a.reshape(a.shape[0] // nb, nb, a.shape[1]), 0, 1)


def kernel(x_prompt, x_sample, state_lru_h, state_lru_conv, state_pool, state_ffn_conv, norm_mix, w_in, conv_lru_w, conv_lru_b, w_ra, b_ra, w_ix, b_ix, lru_lambda, w_pool, pool_scale, w_br_lru, w_br_pool, w_out, norm_ffn, w_up, conv_ffn_w, conv_ffn_b, w_down, norm_final):
    assert norm_mix.shape[0] == 1, "one layer"
    row = lambda v: v.reshape(1, -1)
    w_gate = jnp.concatenate([_block_diag_tiles(w_ra[0]), _block_diag_tiles(w_ix[0])], axis=-1)
    weights = (
        row(norm_mix[0]), w_in[0].astype(_BF16), conv_lru_w[0], row(conv_lru_b[0]),
        w_gate.astype(_BF16), row(b_ra[0]), row(b_ix[0]), row(lru_lambda[0]),
        w_pool[0].astype(_BF16), row(pool_scale[0]),
        w_br_lru[0].astype(_BF16), w_br_pool[0].astype(_BF16), w_out[0].astype(_BF16),
        row(norm_ffn[0]), w_up[0].astype(_BF16), conv_ffn_w[0], row(conv_ffn_b[0]),
        w_down[0].astype(_BF16), row(norm_final),
    )

    def run(x, h0, lru0, pool0, ffn0, offset, tt, name):
        nb = x.shape[0]
        y, h, lru, pool, ffn = _run_stack(
            _to_time_major(x), h0, _to_time_major(lru0), _to_time_major(pool0),
            _to_time_major(ffn0), weights, nb=nb, tt=tt, offset=offset, name=name)
        return (_from_time_major(y, nb), h[None], _from_time_major(lru, nb)[None],
                _from_time_major(pool, nb)[None], _from_time_major(ffn, nb)[None])

    nbp = x_prompt.shape[0]
    dt = x_prompt.dtype
    y_p, h_p, lru_p, pool_p, ffn_p = run(
        x_prompt,
        jnp.zeros((nbp, D_RNN), _F32),
        jnp.zeros((nbp, LRU_CONV - 1, D_RNN), dt),
        jnp.zeros((nbp, POOL_HIST, D_POOL), dt),
        jnp.zeros((nbp, FFN_CONV - 1, 2 * D_FF), dt),
        0, 32, "prompt_layer")
    y_s, h_s, lru_s, pool_s, ffn_s = run(
        x_sample, state_lru_h[0], state_lru_conv[0], state_pool[0], state_ffn_conv[0],
        PAST_LEN, x_sample.shape[1], "sample_layer")
    return (y_p, y_s, h_p, lru_p, pool_p, ffn_p, h_s, lru_s, pool_s, ffn_s)
```

```python
import functools
import math

import jax
import jax.numpy as jnp
from jax import lax
from jax.experimental import pallas as pl
from jax.experimental.pallas import tpu as pltpu

D_MODEL = 1024
D_RNN = D_MODEL
D_POOL = D_MODEL
D_FF = 3 * D_MODEL
N_LRU_BLOCKS = 16
LRU_BLOCK = D_RNN // N_LRU_BLOCKS
LRU_CONV = 4
LRU_C = 8.0
POOL_WINDOWS = (2, 4, 8, 16)
POOL_GROUP = D_POOL // len(POOL_WINDOWS)
POOL_HIST = max(POOL_WINDOWS) - 1
FFN_CONV = 3
EPS = 1e-6
PAST_LEN = 2048

MXU_DIM_V7X = 256
GATE_CHUNKS = D_RNN // MXU_DIM_V7X
VMEM_LIMIT_BYTES = 58 * 1024 * 1024
ROW_CHUNK = 64
FFN_COL_CHUNK = 512
GELU_C = math.sqrt(2.0 / math.pi)

_BF16 = jnp.bfloat16
_F32 = jnp.float32


def _sigmoid(x):
    return 0.5 * jnp.tanh(0.5 * x) + 0.5


def _dot(a, b):
    return jnp.dot(a, b, preferred_element_type=_F32)


def _row_loop(n_rows, chunk, body):
    for i in range(n_rows // chunk):
        body(i * chunk)


def _layer_kernel(
        x_ref, h0_ref, lru0_ref, pool0_ref, ffn0_ref,
        norm_mix_ref, w_in_ref, conv_lru_w_ref, conv_lru_b_ref, w_gate_ref, b_ra_ref, b_ix_ref,
        lam_ref, w_pool_ref, pool_scale_ref, w_br_lru_ref, w_br_pool_ref, w_out_ref,
        norm_ffn_ref, w_up_ref, conv_ffn_w_ref, conv_ffn_b_ref, w_down_ref, norm_final_ref,
        y_ref, h_ref, lru_new_ref, pool_new_ref, ffn_new_ref,
        rnn_buf, pool_buf, gate_buf, hup_buf, xn_buf, act_buf,
        *, nb, tt, offset):
    tm = nb * tt
    lru_hist = (LRU_CONV - 1) * nb
    pool_hist = POOL_HIST * nb
    ffn_hist = (FFN_CONV - 1) * nb
    step = pl.program_id(0)
    rows = ROW_CHUNK

    @pl.when(step == 0)
    def _():
        h_ref[...] = h0_ref[...]
        rnn_buf[0:lru_hist, :] = lru0_ref[...]
        pool_buf[0:pool_hist, :] = pool0_ref[...]
        hup_buf[0:ffn_hist, :] = ffn0_ref[...]

    def tmp(k):
        return hup_buf.at[ffn_hist:ffn_hist + tm, k * D_MODEL:(k + 1) * D_MODEL]
    u_tmp, r_tmp, i_tmp, a_tmp, b_tmp, h_tmp = (tmp(k) for k in range(6))
    ya_tmp, yb_tmp = u_tmp, r_tmp

    g_mix = norm_mix_ref[...]

    def norm_mix_body(r0):
        x = x_ref[pl.ds(r0, rows), :]
        ms = jnp.mean(x * x, axis=-1, keepdims=True)
        xn_buf[pl.ds(r0, rows), :] = ((x * lax.rsqrt(ms + EPS)) * g_mix).astype(_BF16)
    _row_loop(tm, rows, norm_mix_body)

    xn = xn_buf[...]
    rnn_buf[lru_hist:, :] = _dot(xn, w_in_ref[:, 0:D_RNN])
    pool_buf[pool_hist:, :] = _dot(xn, w_in_ref[:, D_RNN:D_RNN + D_POOL])
    gate_buf[...] = _dot(xn, w_in_ref[:, D_RNN + D_POOL:])

    def lru_conv_body(r0):
        u = rnn_buf[pl.ds(r0, rows), :] * conv_lru_w_ref[0:1, :]
        for k in range(1, LRU_CONV):
            u = u + rnn_buf[pl.ds(r0 + k * nb, rows), :] * conv_lru_w_ref[k:k + 1, :]
        u = u + conv_lru_b_ref[...]
        u_tmp[pl.ds(r0, rows), :] = u
        xn_buf[pl.ds(r0, rows), :] = u.astype(_BF16)
    _row_loop(tm, rows, lru_conv_body)

    for c in range(GATE_CHUNKS):
        cols = slice(c * MXU_DIM_V7X, (c + 1) * MXU_DIM_V7X)
        ri = _dot(xn_buf[:, cols], w_gate_ref[c])
        r_tmp[:, cols] = ri[:, :MXU_DIM_V7X]
        i_tmp[:, cols] = ri[:, MXU_DIM_V7X:]

    lam = lam_ref[...]
    softplus_neg_lam = jnp.maximum(-lam, 0.0) + jnp.log1p(jnp.exp(-jnp.abs(lam)))
    half_log_a_scale = (-0.5 * LRU_C) * softplus_neg_lam

    def lru_coef_body(r0):
        sl = pl.ds(r0, rows)
        u = u_tmp[sl, :]
        r = _sigmoid(r_tmp[sl, :] + b_ra_ref[...])
        i = _sigmoid(i_tmp[sl, :] + b_ix_ref[...])
        s = jnp.tanh(r * half_log_a_scale)
        q = 1.0 / (1.0 - s)
        a_tmp[sl, :] = (1.0 + s) * q
        b_tmp[sl, :] = ((2.0 * jnp.sqrt(-s)) * q) * (i * u)
    _row_loop(tm, rows, lru_coef_body)

    def scan_body(t, h):
        sl = pl.ds(pl.multiple_of(t * nb, nb), nb)
        h = a_tmp[sl, :] * h + b_tmp[sl, :]
        h_tmp[sl, :] = h
        return h
    h_ref[...] = lax.fori_loop(0, tt, scan_body, h_ref[...], unroll=4)

    ya_tmp[...] = _dot(h_tmp[...].astype(_BF16), w_br_lru_ref[...])

    row_t = lax.broadcasted_iota(jnp.int32, (tm, 128), 0) // nb
    pos1 = row_t + (offset + 1) + step * tt
    for g, w in enumerate(POOL_WINDOWS):
        cols = slice(g * POOL_GROUP, (g + 1) * POOL_GROUP)
        first = POOL_HIST + 1 - w
        s = pool_buf[first * nb:(first + w - 1) * nb + tm, cols]
        span = 1
        while span < w:
            s = s[span * nb:, :] + s[:s.shape[0] - span * nb, :]
            span *= 2
        cnt = jnp.minimum(pos1, w).astype(_F32)
        cnt = jnp.concatenate([cnt] * (POOL_GROUP // 128), axis=1)
        pooled = s / cnt - pool_buf[pool_hist:, cols]
        pp = _dot(pooled.astype(_BF16), w_pool_ref[g]) * pool_scale_ref[:, cols]
        xn_buf[:, cols] = pp.astype(_BF16)
    yb_tmp[...] = _dot(xn_buf[...], w_br_pool_ref[...])

    def merge_body(r0):
        sl = pl.ds(r0, rows)
        m = (_sigmoid(gate_buf[sl, 0:D_MODEL]) * ya_tmp[sl, :]
             + _sigmoid(gate_buf[sl, D_MODEL:]) * yb_tmp[sl, :])
        xn_buf[sl, :] = m.astype(_BF16)
    _row_loop(tm, rows, merge_body)
    y_ref[...] = x_ref[...] + _dot(xn_buf[...], w_out_ref[...])

    g_ffn = norm_ffn_ref[...]

    def norm_ffn_body(r0):
        x = y_ref[pl.ds(r0, rows), :]
        ms = jnp.mean(x * x, axis=-1, keepdims=True)
        xn_buf[pl.ds(r0, rows), :] = ((x * lax.rsqrt(ms + EPS)) * g_ffn).astype(_BF16)
    _row_loop(tm, rows, norm_ffn_body)

    xn = xn_buf[...]
    for c in range(3):
        cols = slice(c * 2 * D_MODEL, (c + 1) * 2 * D_MODEL)
        hup_buf[ffn_hist:, cols] = _dot(xn, w_up_ref[:, cols])

    def conv_cols(r0, cols):
        y = hup_buf[pl.ds(r0, rows), cols] * conv_ffn_w_ref[0:1, cols]
        for k in range(1, FFN_CONV):
            y = y + hup_buf[pl.ds(r0 + k * nb, rows), cols] * conv_ffn_w_ref[k:k + 1, cols]
        return y + conv_ffn_b_ref[:, cols]

    def geglu_body(r0):
        for c in range(D_FF // FFN_COL_CHUNK):
            gcols = slice(c * FFN_COL_CHUNK, (c + 1) * FFN_COL_CHUNK)
            vcols = slice(D_FF + c * FFN_COL_CHUNK, D_FF + (c + 1) * FFN_COL_CHUNK)
            gt = conv_cols(r0, gcols)
            v = conv_cols(r0, vcols)
            cdf = 0.5 * (1.0 + jnp.tanh(GELU_C * (gt + 0.044715 * (gt * gt * gt))))
            act_buf[pl.ds(r0, rows), gcols] = ((gt * cdf) * v).astype(_BF16)
    _row_loop(tm, rows, geglu_body)

    y_ref[...] = y_ref[...] + _dot(act_buf[...], w_down_ref[...])

    g_final = norm_final_ref[...]

    def norm_final_body(r0):
        x = y_ref[pl.ds(r0, rows), :]
        ms = jnp.mean(x * x, axis=-1, keepdims=True)
        y_ref[pl.ds(r0, rows), :] = (x * lax.rsqrt(ms + EPS)) * g_final
    _row_loop(tm, rows, norm_final_body)

    rnn_buf[0:lru_hist, :] = rnn_buf[tm:tm + lru_hist, :]
    pool_buf[0:pool_hist, :] = pool_buf[tm:tm + pool_hist, :]
    hup_buf[0:ffn_hist, :] = hup_buf[tm:tm + ffn_hist, :]

    @pl.when(step == pl.num_programs(0) - 1)
    def _():
        lru_new_ref[...] = rnn_buf[0:lru_hist, :]
        pool_new_ref[...] = pool_buf[0:pool_hist, :]
        ffn_new_ref[...] = hup_buf[0:ffn_hist, :]


def _full_spec(a):
    zeros = (0,) * a.ndim
    return pl.BlockSpec(a.shape, lambda i: zeros)


def _run_stack(x_tb, h0, lru0, pool0, ffn0, weights, *, nb, tt, offset, name):
    n_rows = x_tb.shape[0]
    tm = nb * tt
    assert n_rows % tm == 0 and tm % ROW_CHUNK == 0 and tt > POOL_HIST
    small = (h0, lru0, pool0, ffn0)
    out_shape = (
        jax.ShapeDtypeStruct((n_rows, D_MODEL), _F32),
        jax.ShapeDtypeStruct(h0.shape, _F32),
        jax.ShapeDtypeStruct(lru0.shape, _F32),
        jax.ShapeDtypeStruct(pool0.shape, _F32),
        jax.ShapeDtypeStruct(ffn0.shape, _F32),
    )
    row_spec = pl.BlockSpec((tm, D_MODEL), lambda i: (i, 0))
    return pl.pallas_call(
        functools.partial(_layer_kernel, nb=nb, tt=tt, offset=offset),
        grid=(n_rows // tm,),
        in_specs=[row_spec] + [_full_spec(a) for a in small + tuple(weights)],
        out_specs=(row_spec,) + tuple(_full_spec(a) for a in small),
        out_shape=out_shape,
        scratch_shapes=[
            pltpu.VMEM(((LRU_CONV - 1) * nb + tm, D_RNN), _F32),
            pltpu.VMEM((POOL_HIST * nb + tm, D_POOL), _F32),
            pltpu.VMEM((tm, 2 * D_MODEL), _F32),
            pltpu.VMEM(((FFN_CONV - 1) * nb + tm, 2 * D_FF), _F32),
            pltpu.VMEM((tm, D_MODEL), _BF16),
            pltpu.VMEM((tm, D_FF), _BF16),
        ],
        compiler_params=pltpu.CompilerParams(
            dimension_semantics=("arbitrary",), vmem_limit_bytes=VMEM_LIMIT_BYTES),
        name=name,
    )(x_tb, *small, *weights)


def _block_diag_tiles(w):
    per = MXU_DIM_V7X // LRU_BLOCK
    w4 = w.reshape(GATE_CHUNKS, per, LRU_BLOCK, LRU_BLOCK)
    eye = jnp.eye(per, dtype=w.dtype)
    return jnp.einsum('ciab,ij->ciajb', w4, eye).reshape(GATE_CHUNKS, MXU_DIM_V7X, MXU_DIM_V7X)


def _to_time_major(a):
    return jnp.swapaxes(a, 0, 1).reshape(a.shape[0] * a.shape[1], a.shape[2])


def _from_time_major(a, nb):
    return jnp.swapaxes(a.reshape(a.shape[0] // nb, nb, a.shape[1]), 0, 1)


def kernel(x_prompt, x_sample, state_lru_h, state_lru_conv, state_pool, state_ffn_conv, norm_mix, w_in, conv_lru_w, conv_lru_b, w_ra, b_ra, w_ix, b_ix, lru_lambda, w_pool, pool_scale, w_br_lru, w_br_pool, w_out, norm_ffn, w_up, conv_ffn_w, conv_ffn_b, w_down, norm_final):
    assert norm_mix.shape[0] == 1, "one layer"
    row = lambda v: v.reshape(1, -1)
    w_gate = jnp.concatenate([_block_diag_tiles(w_ra[0]), _block_diag_tiles(w_ix[0])], axis=-1)
    weights = (
        row(norm_mix[0]), w_in[0].astype(_BF16), conv_lru_w[0], row(conv_lru_b[0]),
        w_gate.astype(_BF16), row(b_ra[0]), row(b_ix[0]), row(lru_lambda[0]),
        w_pool[0].astype(_BF16), row(pool_scale[0]),
        w_br_lru[0].astype(_BF16), w_br_pool[0].astype(_BF16), w_out[0].astype(_BF16),
        row(norm_ffn[0]), w_up[0].astype(_BF16), conv_ffn_w[0], row(conv_ffn_b[0]),
        w_down[0].astype(_BF16), row(norm_final),
    )

    def run(x, h0, lru0, pool0, ffn0, offset, tt, name):
        nb = x.shape[0]
        y, h, lru, pool, ffn = _run_stack(
            _to_time_major(x), h0, _to_time_major(lru0), _to_time_major(pool0),
            _to_time_major(ffn0), weights, nb=nb, tt=tt, offset=offset, name=name)
        return (_from_time_major(y, nb), h[None], _from_time_major(lru, nb)[None],
                _from_time_major(pool, nb)[None], _from_time_major(ffn, nb)[None])

    nbp = x_prompt.shape[0]
    dt = x_prompt.dtype
    y_p, h_p, lru_p, pool_p, ffn_p = run(
        x_prompt,
        jnp.zeros((nbp, D_RNN), _F32),
        jnp.zeros((nbp, LRU_CONV - 1, D_RNN), dt),
        jnp.zeros((nbp, POOL_HIST, D_POOL), dt),
        jnp.zeros((nbp, FFN_CONV - 1, 2 * D_FF), dt),
        0, 32, "prompt_layer")
    y_s, h_s, lru_s, pool_s, ffn_s = run(
        x_sample, state_lru_h[0], state_lru_conv[0], state_pool[0], state_ffn_conv[0],
        PAST_LEN, x_sample.shape[1], "sample_layer")
    return (y_p, y_s, h_p, lru_p, pool_p, ffn_p, h_s, lru_s, pool_s, ffn_s)
```

```python
import functools
import math

import jax
import jax.numpy as jnp
from jax import lax
from jax.experimental import pallas as pl
from jax.experimental.pallas import tpu as pltpu

D_MODEL = 1024
D_RNN = D_MODEL
D_POOL = D_MODEL
D_FF = 3 * D_MODEL
N_LRU_BLOCKS = 16
LRU_BLOCK = D_RNN // N_LRU_BLOCKS
LRU_CONV = 4
LRU_C = 8.0
POOL_WINDOWS = (2, 4, 8, 16)
POOL_GROUP = D_POOL // len(POOL_WINDOWS)
POOL_HIST = max(POOL_WINDOWS) - 1
FFN_CONV = 3
EPS = 1e-6
PAST_LEN = 2048

MXU_DIM_V7X = 256
GATE_CHUNKS = D_RNN // MXU_DIM_V7X
VMEM_LIMIT_BYTES = 58 * 1024 * 1024
ROW_CHUNK = 64
FFN_COL_CHUNK = 512
GELU_C = math.sqrt(2.0 / math.pi)

_BF16 = jnp.bfloat16
_F32 = jnp.float32


def _sigmoid(x):
    return 0.5 * jnp.tanh(0.5 * x) + 0.5


def _dot(a, b):
    return jnp.dot(a, b, preferred_element_type=_F32)


def _row_loop(n_rows, chunk, body):
    for i in range(n_rows // chunk):
        body(i * chunk)


def _layer_kernel(
        x_ref, h0_ref, lru0_ref, pool0_ref, ffn0_ref,
        norm_mix_ref, w_in_ref, conv_lru_w_ref, conv_lru_b_ref, w_gate_ref, b_ra_ref, b_ix_ref,
        lam_ref, w_pool_ref, pool_scale_ref, w_br_lru_ref, w_br_pool_ref, w_out_ref,
        norm_ffn_ref, w_up_ref, conv_ffn_w_ref, conv_ffn_b_ref, w_down_ref, norm_final_ref,
        y_ref, h_ref, lru_new_ref, pool_new_ref, ffn_new_ref,
        rnn_buf, pool_buf, gate_buf, hup_buf, xn_buf, act_buf, res_buf,
        *, nb, tt, offset):
    tm = nb * tt
    lru_hist = (LRU_CONV - 1) * nb
    pool_hist = POOL_HIST * nb
    ffn_hist = (FFN_CONV - 1) * nb
    step = pl.program_id(0)
    rows = ROW_CHUNK

    @pl.when(step == 0)
    def _():
        h_ref[...] = h0_ref[...]
        rnn_buf[0:lru_hist, :] = lru0_ref[...]
        pool_buf[0:pool_hist, :] = pool0_ref[...]
        hup_buf[0:ffn_hist, :] = ffn0_ref[...]

    def tmp(k):
        return hup_buf.at[ffn_hist:ffn_hist + tm, k * D_MODEL:(k + 1) * D_MODEL]
    u_tmp, r_tmp, i_tmp, a_tmp, b_tmp, yb_tmp = (tmp(k) for k in range(6))
    ya_tmp, h_tmp = u_tmp, b_tmp

    for t0 in range(0, tt, 8):
        blk = jnp.swapaxes(x_ref[:, t0:t0 + 8, :], 0, 1)
        res_buf[t0 * nb:(t0 + 8) * nb, :] = blk.reshape(8 * nb, D_MODEL)

    g_mix = norm_mix_ref[...]

    def norm_mix_body(r0):
        x = res_buf[pl.ds(r0, rows), :]
        ms = jnp.mean(x * x, axis=-1, keepdims=True)
        xn_buf[pl.ds(r0, rows), :] = ((x * lax.rsqrt(ms + EPS)) * g_mix).astype(_BF16)
    _row_loop(tm, rows, norm_mix_body)

    xn = xn_buf[...]
    rnn_buf[lru_hist:, :] = _dot(xn, w_in_ref[:, 0:D_RNN])
    pool_buf[pool_hist:, :] = _dot(xn, w_in_ref[:, D_RNN:D_RNN + D_POOL])
    gate_buf[...] = _dot(xn, w_in_ref[:, D_RNN + D_POOL:])

    def lru_conv_body(r0):
        u = rnn_buf[pl.ds(r0, rows), :] * conv_lru_w_ref[0:1, :]
        for k in range(1, LRU_CONV):
            u = u + rnn_buf[pl.ds(r0 + k * nb, rows), :] * conv_lru_w_ref[k:k + 1, :]
        u = u + conv_lru_b_ref[...]
        u_tmp[pl.ds(r0, rows), :] = u
        xn_buf[pl.ds(r0, rows), :] = u.astype(_BF16)
    _row_loop(tm, rows, lru_conv_body)

    for c in range(GATE_CHUNKS):
        cols = slice(c * MXU_DIM_V7X, (c + 1) * MXU_DIM_V7X)
        ri = _dot(xn_buf[:, cols], w_gate_ref[c])
        r_tmp[:, cols] = ri[:, :MXU_DIM_V7X]
        i_tmp[:, cols] = ri[:, MXU_DIM_V7X:]

    row_t = lax.broadcasted_iota(jnp.int32, (tm, 128), 0) // nb
    pos1 = row_t + (offset + 1) + step * tt
    pp_buf = act_buf.at[:, 0:D_POOL]
    for g, w in enumerate(POOL_WINDOWS):
        cols = slice(g * POOL_GROUP, (g + 1) * POOL_GROUP)
        first = POOL_HIST + 1 - w
        s = pool_buf[first * nb:(first + w - 1) * nb + tm, cols]
        span = 1
        while span < w:
            s = s[span * nb:, :] + s[:s.shape[0] - span * nb, :]
            span *= 2
        cnt = jnp.minimum(pos1, w).astype(_F32)
        cnt = jnp.concatenate([cnt] * (POOL_GROUP // 128), axis=1)
        pooled = s / cnt - pool_buf[pool_hist:, cols]
        pp = _dot(pooled.astype(_BF16), w_pool_ref[g]) * pool_scale_ref[:, cols]
        pp_buf[:, cols] = pp.astype(_BF16)
    yb_tmp[...] = _dot(pp_buf[...], w_br_pool_ref[...])

    lam = lam_ref[...]
    softplus_neg_lam = jnp.maximum(-lam, 0.0) + jnp.log1p(jnp.exp(-jnp.abs(lam)))
    half_log_a_scale = (-0.5 * LRU_C) * softplus_neg_lam

    def lru_coef_body(r0):
        sl = pl.ds(r0, rows)
        u = u_tmp[sl, :]
        r = _sigmoid(r_tmp[sl, :] + b_ra_ref[...])
        i = _sigmoid(i_tmp[sl, :] + b_ix_ref[...])
        s = jnp.tanh(r * half_log_a_scale)
        q = 1.0 / (1.0 - s)
        a_tmp[sl, :] = (1.0 + s) * q
        b_tmp[sl, :] = ((2.0 * jnp.sqrt(-s)) * q) * (i * u)
    _row_loop(tm, rows, lru_coef_body)

    h = h_ref[...]
    for t in range(tt):
        sl = slice(t * nb, (t + 1) * nb)
        h = a_tmp[sl, :] * h + b_tmp[sl, :]
        h_tmp[sl, :] = h
    h_ref[...] = h

    ya_tmp[...] = _dot(h_tmp[...].astype(_BF16), w_br_lru_ref[...])

    def merge_body(r0):
        sl = pl.ds(r0, rows)
        m = (_sigmoid(gate_buf[sl, 0:D_MODEL]) * ya_tmp[sl, :]
             + _sigmoid(gate_buf[sl, D_MODEL:]) * yb_tmp[sl, :])
        xn_buf[sl, :] = m.astype(_BF16)
    _row_loop(tm, rows, merge_body)
    res_buf[...] = res_buf[...] + _dot(xn_buf[...], w_out_ref[...])

    g_ffn = norm_ffn_ref[...]

    def norm_ffn_body(r0):
        x = res_buf[pl.ds(r0, rows), :]
        ms = jnp.mean(x * x, axis=-1, keepdims=True)
        xn_buf[pl.ds(r0, rows), :] = ((x * lax.rsqrt(ms + EPS)) * g_ffn).astype(_BF16)
    _row_loop(tm, rows, norm_ffn_body)

    xn = xn_buf[...]
    for c in range(3):
        cols = slice(c * 2 * D_MODEL, (c + 1) * 2 * D_MODEL)
        hup_buf[ffn_hist:, cols] = _dot(xn, w_up_ref[:, cols])

    def conv_cols(r0, cols):
        y = hup_buf[pl.ds(r0, rows), cols] * conv_ffn_w_ref[0:1, cols]
        for k in range(1, FFN_CONV):
            y = y + hup_buf[pl.ds(r0 + k * nb, rows), cols] * conv_ffn_w_ref[k:k + 1, cols]
        return y + conv_ffn_b_ref[:, cols]

    def geglu_body(r0):
        for c in range(D_FF // FFN_COL_CHUNK):
            gcols = slice(c * FFN_COL_CHUNK, (c + 1) * FFN_COL_CHUNK)
            vcols = slice(D_FF + c * FFN_COL_CHUNK, D_FF + (c + 1) * FFN_COL_CHUNK)
            gt = conv_cols(r0, gcols)
            v = conv_cols(r0, vcols)
            cdf = 0.5 * (1.0 + jnp.tanh(GELU_C * (gt + 0.044715 * (gt * gt * gt))))
            act_buf[pl.ds(r0, rows), gcols] = ((gt * cdf) * v).astype(_BF16)
    _row_loop(tm, rows, geglu_body)

    res_buf[...] = res_buf[...] + _dot(act_buf[...], w_down_ref[...])

    g_final = norm_final_ref[...]

    def norm_final_body(r0):
        x = res_buf[pl.ds(r0, rows), :]
        ms = jnp.mean(x * x, axis=-1, keepdims=True)
        res_buf[pl.ds(r0, rows), :] = (x * lax.rsqrt(ms + EPS)) * g_final
    _row_loop(tm, rows, norm_final_body)
    for t0 in range(0, tt, 8):
        blk = res_buf[t0 * nb:(t0 + 8) * nb, :].reshape(8, nb, D_MODEL)
        y_ref[:, t0:t0 + 8, :] = jnp.swapaxes(blk, 0, 1)

    rnn_buf[0:lru_hist, :] = rnn_buf[tm:tm + lru_hist, :]
    pool_buf[0:pool_hist, :] = pool_buf[tm:tm + pool_hist, :]
    hup_buf[0:ffn_hist, :] = hup_buf[tm:tm + ffn_hist, :]

    @pl.when(step == pl.num_programs(0) - 1)
    def _():
        lru_new_ref[...] = rnn_buf[0:lru_hist, :]
        pool_new_ref[...] = pool_buf[0:pool_hist, :]
        ffn_new_ref[...] = hup_buf[0:ffn_hist, :]


def _full_spec(a):
    zeros = (0,) * a.ndim
    return pl.BlockSpec(a.shape, lambda i: zeros)


def _run_stack(x, h0, lru0, pool0, ffn0, weights, *, tt, offset, name):
    nb, t_len, _ = x.shape
    tm = nb * tt
    assert t_len % tt == 0 and tt % 8 == 0 and tm % ROW_CHUNK == 0 and tt > POOL_HIST
    small = (h0, lru0, pool0, ffn0)
    out_shape = (
        jax.ShapeDtypeStruct(x.shape, _F32),
        jax.ShapeDtypeStruct(h0.shape, _F32),
        jax.ShapeDtypeStruct(lru0.shape, _F32),
        jax.ShapeDtypeStruct(pool0.shape, _F32),
        jax.ShapeDtypeStruct(ffn0.shape, _F32),
    )
    row_spec = pl.BlockSpec((nb, tt, D_MODEL), lambda i: (0, i, 0))
    return pl.pallas_call(
        functools.partial(_layer_kernel, nb=nb, tt=tt, offset=offset),
        grid=(t_len // tt,),
        in_specs=[row_spec] + [_full_spec(a) for a in small + tuple(weights)],
        out_specs=(row_spec,) + tuple(_full_spec(a) for a in small),
        out_shape=out_shape,
        scratch_shapes=[
            pltpu.VMEM(((LRU_CONV - 1) * nb + tm, D_RNN), _F32),
            pltpu.VMEM((POOL_HIST * nb + tm, D_POOL), _F32),
            pltpu.VMEM((tm, 2 * D_MODEL), _F32),
            pltpu.VMEM(((FFN_CONV - 1) * nb + tm, 2 * D_FF), _F32),
            pltpu.VMEM((tm, D_MODEL), _BF16),
            pltpu.VMEM((tm, D_FF), _BF16),
            pltpu.VMEM((tm, D_MODEL), _F32),
        ],
        compiler_params=pltpu.CompilerParams(
            dimension_semantics=("arbitrary",), vmem_limit_bytes=VMEM_LIMIT_BYTES),
        name=name,
    )(x, *small, *weights)


def _block_diag_tiles(w):
    per = MXU_DIM_V7X // LRU_BLOCK
    w4 = w.reshape(GATE_CHUNKS, per, LRU_BLOCK, LRU_BLOCK)
    eye = jnp.eye(per, dtype=w.dtype)
    return jnp.einsum('ciab,ij->ciajb', w4, eye).reshape(GATE_CHUNKS, MXU_DIM_V7X, MXU_DIM_V7X)


def _to_time_major(a):
    return jnp.swapaxes(a, 0, 1).reshape(a.shape[0] * a.shape[1], a.shape[2])


def _from_time_major(a, nb):
    return jnp.swapaxes(a.reshape(a.shape[0] // nb, nb, a.shape[1]), 0, 1)


def kernel(x_prompt, x_sample, state_lru_h, state_lru_conv, state_pool, state_ffn_conv, norm_mix, w_in, conv_lru_w, conv_lru_b, w_ra, b_ra, w_ix, b_ix, lru_lambda, w_pool, pool_scale, w_br_lru, w_br_pool, w_out, norm_ffn, w_up, conv_ffn_w, conv_ffn_b, w_down, norm_final):
    assert norm_mix.shape[0] == 1, "one layer"
    row = lambda v: v.reshape(1, -1)
    w_gate = jnp.concatenate([_block_diag_tiles(w_ra[0]), _block_diag_tiles(w_ix[0])], axis=-1)
    weights = (
        row(norm_mix[0]), w_in[0].astype(_BF16), conv_lru_w[0], row(conv_lru_b[0]),
        w_gate.astype(_BF16), row(b_ra[0]), row(b_ix[0]), row(lru_lambda[0]),
        w_pool[0].astype(_BF16), row(pool_scale[0]),
        w_br_lru[0].astype(_BF16), w_br_pool[0].astype(_BF16), w_out[0].astype(_BF16),
        row(norm_ffn[0]), w_up[0].astype(_BF16), conv_ffn_w[0], row(conv_ffn_b[0]),
        w_down[0].astype(_BF16), row(norm_final),
    )

    def run(x, h0, lru0, pool0, ffn0, offset, tt, name):
        nb = x.shape[0]
        y, h, lru, pool, ffn = _run_stack(
            x, h0, _to_time_major(lru0), _to_time_major(pool0),
            _to_time_major(ffn0), weights, tt=tt, offset=offset, name=name)
        return (y, h[None], _from_time_major(lru, nb)[None],
                _from_time_major(pool, nb)[None], _from_time_major(ffn, nb)[None])

    nbp = x_prompt.shape[0]
    dt = x_prompt.dtype
    y_p, h_p, lru_p, pool_p, ffn_p = run(
        x_prompt,
        jnp.zeros((nbp, D_RNN), _F32),
        jnp.zeros((nbp, LRU_CONV - 1, D_RNN), dt),
        jnp.zeros((nbp, POOL_HIST, D_POOL), dt),
        jnp.zeros((nbp, FFN_CONV - 1, 2 * D_FF), dt),
        0, 32, "prompt_layer")
    y_s, h_s, lru_s, pool_s, ffn_s = run(
        x_sample, state_lru_h[0], state_lru_conv[0], state_pool[0], state_ffn_conv[0],
        PAST_LEN, x_sample.shape[1], "sample_layer")
    return (y_p, y_s, h_p, lru_p, pool_p, ffn_p, h_s, lru_s, pool_s, ffn_s)
```

```python
import functools
import math

import jax
import jax.numpy as jnp
from jax import lax
from jax.experimental import pallas as pl
from jax.experimental.pallas import tpu as pltpu

D_MODEL = 1024
D_RNN = D_MODEL
D_POOL = D_MODEL
D_FF = 3 * D_MODEL
N_LRU_BLOCKS = 16
LRU_BLOCK = D_RNN // N_LRU_BLOCKS
LRU_CONV = 4
LRU_C = 8.0
POOL_WINDOWS = (2, 4, 8, 16)
POOL_GROUP = D_POOL // len(POOL_WINDOWS)
POOL_HIST = max(POOL_WINDOWS) - 1
FFN_CONV = 3
EPS = 1e-6
PAST_LEN = 2048

MXU_DIM_V7X = 256
GATE_CHUNKS = D_RNN // MXU_DIM_V7X
VMEM_LIMIT_BYTES = 58 * 1024 * 1024
ROW_CHUNK = 64
FFN_COL_CHUNK = 512
PROMPT_TT = 32
GELU_C1 = math.sqrt(2.0 / math.pi)
GELU_C2 = GELU_C1 * 0.044715

_BF16 = jnp.bfloat16
_F32 = jnp.float32


def _sigmoid(x):
    return 0.5 * jnp.tanh(0.5 * x) + 0.5


def _dot(a, w_packed):
    return jnp.dot(a, pltpu.bitcast(w_packed, _BF16), preferred_element_type=_F32)


def _pack_weight(w):
    wb = w.astype(_BF16)
    *lead, k, n = wb.shape
    pairs = jnp.swapaxes(wb.reshape(*lead, k // 2, 2, n), -1, -2)
    return lax.bitcast_convert_type(pairs, jnp.uint32)


def _rms_scale(x):
    return x * lax.rsqrt(jnp.mean(x * x, axis=-1, keepdims=True) + EPS)


def _zero_after(pieces):
    acc = None
    for p in pieces:
        bits = pltpu.bitcast(p, jnp.uint32)
        z = lax.shift_right_logical(lax.shift_right_logical(bits, jnp.uint32(16)), jnp.uint32(16))
        acc = z if acc is None else acc | z
    return pltpu.bitcast(acc, _F32)[0:1, :]


def _mixer_kernel(
        x_first_ref, x_next_ref, h0_ref, lru0_ref, pool0_ref,
        norm_mix_ref, w_in_ref, conv_lru_w_ref, conv_lru_b_ref, w_gate_ref, b_ra_ref, b_ix_ref,
        lam_ref, w_pool_ref, pool_scale_ref, w_br_lru_ref, w_br_pool_ref, w_out_ref,
        x1_ref, h_ref, lru_new_ref, pool_new_ref,
        rnn_buf, pool_buf, gate_buf, tmp_buf, xn_buf, u_bf, pp_buf, res_pre, xn_pre,
        *, nb, tt, offset):
    tm = nb * tt
    lru_hist = (LRU_CONV - 1) * nb
    pool_hist = POOL_HIST * nb
    step = pl.program_id(0)
    rows = ROW_CHUNK
    row_chunks = [slice(r0, r0 + rows) for r0 in range(0, tm, rows)]
    g_mix = norm_mix_ref[...]

    def load_norm(x_ref):
        for t0 in range(0, tt, 8):
            blk = jnp.swapaxes(x_ref[:, t0:t0 + 8, :], 0, 1)
            res_pre[t0 * nb:(t0 + 8) * nb, :] = blk.reshape(8 * nb, D_MODEL)
        for sl in row_chunks:
            xn_pre[sl, :] = (_rms_scale(res_pre[sl, :]) * g_mix).astype(_BF16)

    @pl.when(step == 0)
    def _():
        h_ref[...] = h0_ref[...]
        rnn_buf[0:lru_hist, :] = lru0_ref[...]
        pool_buf[0:pool_hist, :] = pool0_ref[...]
        load_norm(x_first_ref)

    def tmp(k):
        return tmp_buf.at[:, k * D_MODEL:(k + 1) * D_MODEL]
    u_tmp, r_tmp, i_tmp, a_tmp, b_tmp, yb_tmp = (tmp(k) for k in range(6))
    ya_tmp, h_tmp = u_tmp, b_tmp

    x1_ref[...] = res_pre[...]
    xn_buf[...] = xn_pre[...]
    xn = xn_buf[...]
    rnn_buf[lru_hist:, :] = _dot(xn, w_in_ref[:, 0:D_RNN])

    load_norm(x_next_ref)
    prepared = _zero_after([xn_pre[sl.start:sl.start + 16, 0:128] for sl in row_chunks])
    conv_lru_b = conv_lru_b_ref[...] + jnp.concatenate([prepared] * (D_RNN // 128), axis=1)

    pool_buf[pool_hist:, :] = _dot(xn, w_in_ref[:, D_RNN:D_RNN + D_POOL])

    for sl in row_chunks:
        u = rnn_buf[sl, :] * conv_lru_w_ref[0:1, :]
        for k in range(1, LRU_CONV):
            u = u + rnn_buf[sl.start + k * nb:sl.stop + k * nb, :] * conv_lru_w_ref[k:k + 1, :]
        u = u + conv_lru_b
        u_tmp[sl, :] = u
        u_bf[sl, :] = u.astype(_BF16)

    gate_cols = D_RNN + D_POOL
    gate_buf[:, 0:D_MODEL] = _dot(xn, w_in_ref[:, gate_cols:gate_cols + D_MODEL])

    for c in range(GATE_CHUNKS):
        cols = slice(c * MXU_DIM_V7X, (c + 1) * MXU_DIM_V7X)
        ri = _dot(u_bf[:, cols], w_gate_ref[c])
        r_tmp[:, cols] = ri[:, :MXU_DIM_V7X]
        i_tmp[:, cols] = ri[:, MXU_DIM_V7X:]

    gate_buf[:, D_MODEL:] = _dot(xn, w_in_ref[:, gate_cols + D_MODEL:])

    row_t = lax.broadcasted_iota(jnp.int32, (tm, 128), 0) // nb
    pos1 = row_t + (offset + 1) + step * tt
    for g, w in enumerate(POOL_WINDOWS):
        cols = slice(g * POOL_GROUP, (g + 1) * POOL_GROUP)
        first = POOL_HIST + 1 - w
        s = pool_buf[first * nb:(first + w - 1) * nb + tm, cols]
        span = 1
        while span < w:
            s = s[span * nb:, :] + s[:s.shape[0] - span * nb, :]
            span *= 2
        cnt = jnp.minimum(pos1, w).astype(_F32)
        cnt = jnp.concatenate([cnt] * (POOL_GROUP // 128), axis=1)
        pooled = s / cnt - pool_buf[pool_hist:, cols]
        pp = _dot(pooled.astype(_BF16), w_pool_ref[g]) * pool_scale_ref[:, cols]
        pp_buf[:, cols] = pp.astype(_BF16)

    lam = lam_ref[...]
    softplus_neg_lam = jnp.maximum(-lam, 0.0) + jnp.log1p(jnp.exp(-jnp.abs(lam)))
    quarter_scale = (-0.25 * LRU_C) * softplus_neg_lam
    half_b_ra = 0.5 * b_ra_ref[...]
    half_b_ix = 0.5 * b_ix_ref[...]

    for sl in row_chunks:
        tr = jnp.tanh(r_tmp[sl, :] + half_b_ra)
        ti = jnp.tanh(i_tmp[sl, :] + half_b_ix)
        z = jnp.tanh(tr * quarter_scale + quarter_scale)
        q = 1.0 / (1.0 - z)
        a_tmp[sl, :] = (1.0 + z) * q
        b_tmp[sl, :] = (jnp.sqrt(-z) * q) * ((ti + 1.0) * u_tmp[sl, :])
    yb_tmp[...] = _dot(pp_buf[...], w_br_pool_ref[...])

    h = h_ref[...]
    for t in range(tt):
        sl = slice(t * nb, (t + 1) * nb)
        h = a_tmp[sl, :] * h + b_tmp[sl, :]
        h_tmp[sl, :] = h
    h_ref[...] = h

    ya_tmp[...] = _dot(h_tmp[...].astype(_BF16), w_br_lru_ref[...])

    for sl in row_chunks:
        m = (_sigmoid(gate_buf[sl, 0:D_MODEL]) * ya_tmp[sl, :]
             + _sigmoid(gate_buf[sl, D_MODEL:]) * yb_tmp[sl, :])
        xn_buf[sl, :] = m.astype(_BF16)
    x1_ref[...] = x1_ref[...] + _dot(xn_buf[...], w_out_ref[...])

    rnn_buf[0:lru_hist, :] = rnn_buf[tm:tm + lru_hist, :]
    pool_buf[0:pool_hist, :] = pool_buf[tm:tm + pool_hist, :]

    @pl.when(step == pl.num_programs(0) - 1)
    def _():
        lru_new_ref[...] = rnn_buf[0:lru_hist, :]
        pool_new_ref[...] = pool_buf[0:pool_hist, :]


def _ffn_kernel(
        x1_ref, ffn0_ref, norm_ffn_ref, w_up_ref, conv_ffn_w_ref, conv_ffn_b_ref, w_down_ref,
        norm_final_ref,
        y_ref, ffn_new_ref,
        hup_buf, xn_buf, act_buf, res_buf,
        *, nb, tt):
    tm = nb * tt
    ffn_hist = (FFN_CONV - 1) * nb
    step = pl.program_id(0)
    rows = ROW_CHUNK
    row_chunks = [slice(r0, r0 + rows) for r0 in range(0, tm, rows)]

    @pl.when(step == 0)
    def _():
        hup_buf[0:ffn_hist, :] = ffn0_ref[...]

    g_ffn = norm_ffn_ref[...]
    for sl in row_chunks:
        xn_buf[sl, :] = (_rms_scale(x1_ref[sl, :]) * g_ffn).astype(_BF16)
    xn = xn_buf[...]
    for c in range(3):
        cols = slice(c * 2 * D_MODEL, (c + 1) * 2 * D_MODEL)
        hup_buf[ffn_hist:, cols] = _dot(xn, w_up_ref[:, cols])

    def conv_cols(sl, cols, scale):
        y = hup_buf[sl, cols] * (scale * conv_ffn_w_ref[0:1, cols])
        for k in range(1, FFN_CONV):
            taps = hup_buf[sl.start + k * nb:sl.stop + k * nb, cols]
            y = y + taps * (scale * conv_ffn_w_ref[k:k + 1, cols])
        return y + scale * conv_ffn_b_ref[:, cols]

    for sl in row_chunks:
        for c in range(D_FF // FFN_COL_CHUNK):
            gcols = slice(c * FFN_COL_CHUNK, (c + 1) * FFN_COL_CHUNK)
            vcols = slice(D_FF + c * FFN_COL_CHUNK, D_FF + (c + 1) * FFN_COL_CHUNK)
            gt = conv_cols(sl, gcols, 1.0)
            half_v = conv_cols(sl, vcols, 0.5)
            th = jnp.tanh(gt * (GELU_C2 * (gt * gt) + GELU_C1))
            act_buf[sl, gcols] = ((gt * (1.0 + th)) * half_v).astype(_BF16)

    res_buf[...] = x1_ref[...] + _dot(act_buf[...], w_down_ref[...])

    g_final = norm_final_ref[...]
    for t0 in range(0, tt, 8):
        sl = slice(t0 * nb, (t0 + 8) * nb)
        blk = (_rms_scale(res_buf[sl, :]) * g_final).reshape(8, nb, D_MODEL)
        y_ref[:, t0:t0 + 8, :] = jnp.swapaxes(blk, 0, 1)

    hup_buf[0:ffn_hist, :] = hup_buf[tm:tm + ffn_hist, :]

    @pl.when(step == pl.num_programs(0) - 1)
    def _():
        ffn_new_ref[...] = hup_buf[0:ffn_hist, :]


def _full_spec(a):
    zeros = (0,) * a.ndim
    return pl.BlockSpec(a.shape, lambda i: zeros)


def _compiler_params():
    return pltpu.CompilerParams(
        dimension_semantics=("arbitrary",), vmem_limit_bytes=VMEM_LIMIT_BYTES)


def _run_mixer(x, h0, lru0, pool0, weights, *, tt, offset, name):
    nb, t_len, _ = x.shape
    tm = nb * tt
    n_tiles = t_len // tt
    assert t_len % tt == 0 and tt % 8 == 0 and tm % ROW_CHUNK == 0 and tt > POOL_HIST
    small = (h0, lru0, pool0)
    x_first = x[:, :tt, :]
    return pl.pallas_call(
        functools.partial(_mixer_kernel, nb=nb, tt=tt, offset=offset),
        grid=(n_tiles,),
        in_specs=[_full_spec(x_first),
                  pl.BlockSpec((nb, tt, D_MODEL),
                               lambda i: (0, jnp.minimum(i + 1, n_tiles - 1), 0))]
        + [_full_spec(a) for a in small + tuple(weights)],
        out_specs=(pl.BlockSpec((tm, D_MODEL), lambda i: (i, 0)),)
        + tuple(_full_spec(a) for a in small),
        out_shape=(jax.ShapeDtypeStruct((t_len * nb, D_MODEL), _F32),)
        + tuple(jax.ShapeDtypeStruct(a.shape, _F32) for a in small),
        scratch_shapes=[
            pltpu.VMEM(((LRU_CONV - 1) * nb + tm, D_RNN), _F32),
            pltpu.VMEM((POOL_HIST * nb + tm, D_POOL), _F32),
            pltpu.VMEM((tm, 2 * D_MODEL), _F32),
            pltpu.VMEM((tm, 6 * D_MODEL), _F32),
            pltpu.VMEM((tm, D_MODEL), _BF16),
            pltpu.VMEM((tm, D_RNN), _BF16),
            pltpu.VMEM((tm, D_POOL), _BF16),
            pltpu.VMEM((tm, D_MODEL), _F32),
            pltpu.VMEM((tm, D_MODEL), _BF16),
        ],
        compiler_params=_compiler_params(),
        name=name,
    )(x_first, x, *small, *weights)


def _run_ffn(x1, ffn0, weights, *, nb, tt, name):
    tm = nb * tt
    t_len = x1.shape[0] // nb
    return pl.pallas_call(
        functools.partial(_ffn_kernel, nb=nb, tt=tt),
        grid=(t_len // tt,),
        in_specs=[pl.BlockSpec((tm, D_MODEL), lambda i: (i, 0)), _full_spec(ffn0)]
        + [_full_spec(a) for a in weights],
        out_specs=(pl.BlockSpec((nb, tt, D_MODEL), lambda i: (0, i, 0)), _full_spec(ffn0)),
        out_shape=(jax.ShapeDtypeStruct((nb, t_len, D_MODEL), _F32),
                   jax.ShapeDtypeStruct(ffn0.shape, _F32)),
        scratch_shapes=[
            pltpu.VMEM(((FFN_CONV - 1) * nb + tm, 2 * D_FF), _F32),
            pltpu.VMEM((tm, D_MODEL), _BF16),
            pltpu.VMEM((tm, D_FF), _BF16),
            pltpu.VMEM((tm, D_MODEL), _F32),
        ],
        compiler_params=_compiler_params(),
        name=name,
    )(x1, ffn0, *weights)


def _block_diag_tiles(w):
    per = MXU_DIM_V7X // LRU_BLOCK
    w4 = w.reshape(GATE_CHUNKS, per, LRU_BLOCK, LRU_BLOCK)
    eye = jnp.eye(per, dtype=w.dtype)
    return jnp.einsum('ciab,ij->ciajb', w4, eye).reshape(GATE_CHUNKS, MXU_DIM_V7X, MXU_DIM_V7X)


def _to_time_major(a):
    return jnp.swapaxes(a, 0, 1).reshape(a.shape[0] * a.shape[1], a.shape[2])


def _from_time_major(a, nb):
    return jnp.swapaxes(a.reshape(a.shape[0] // nb, nb, a.shape[1]), 0, 1)


def kernel(x_prompt, x_sample, state_lru_h, state_lru_conv, state_pool, state_ffn_conv, norm_mix, w_in, conv_lru_w, conv_lru_b, w_ra, b_ra, w_ix, b_ix, lru_lambda, w_pool, pool_scale, w_br_lru, w_br_pool, w_out, norm_ffn, w_up, conv_ffn_w, conv_ffn_b, w_down, norm_final):
    assert norm_mix.shape[0] == 1, "one layer"
    row = lambda v: v.reshape(1, -1)
    w_gate = jnp.concatenate([_block_diag_tiles(w_ra[0]), _block_diag_tiles(w_ix[0])], axis=-1)
    mixer_w = (
        row(norm_mix[0]), _pack_weight(w_in[0]), conv_lru_w[0], row(conv_lru_b[0]),
        _pack_weight(0.5 * w_gate), row(b_ra[0]), row(b_ix[0]), row(lru_lambda[0]),
        _pack_weight(w_pool[0]), row(pool_scale[0]),
        _pack_weight(w_br_lru[0]), _pack_weight(w_br_pool[0]), _pack_weight(w_out[0]),
    )
    ffn_w = (
        row(norm_ffn[0]), _pack_weight(w_up[0]), conv_ffn_w[0], row(conv_ffn_b[0]),
        _pack_weight(w_down[0]), row(norm_final),
    )

    def run(x, h0, lru0, pool0, ffn0, offset, tt, name):
        nb = x.shape[0]
        x1, h, lru, pool = _run_mixer(
            x, h0, _to_time_major(lru0), _to_time_major(pool0), mixer_w,
            tt=tt, offset=offset, name=name + "_mixer")
        y, ffn = _run_ffn(x1, _to_time_major(ffn0), ffn_w, nb=nb, tt=tt, name=name + "_ffn")
        return (y, h[None], _from_time_major(lru, nb)[None],
                _from_time_major(pool, nb)[None], _from_time_major(ffn, nb)[None])

    nbp = x_prompt.shape[0]
    dt = x_prompt.dtype
    y_p, h_p, lru_p, pool_p, ffn_p = run(
        x_prompt,
        jnp.zeros((nbp, D_RNN), _F32),
        jnp.zeros((nbp, LRU_CONV - 1, D_RNN), dt),
        jnp.zeros((nbp, POOL_HIST, D_POOL), dt),
        jnp.zeros((nbp, FFN_CONV - 1, 2 * D_FF), dt),
        0, PROMPT_TT, "prompt")
    y_s, h_s, lru_s, pool_s, ffn_s = run(
        x_sample, state_lru_h[0], state_lru_conv[0], state_pool[0], state_ffn_conv[0],
        PAST_LEN, x_sample.shape[1], "sample")
    return (y_p, y_s, h_p, lru_p, pool_p, ffn_p, h_s, lru_s, pool_s, ffn_s)
```

```python
import functools
import math

import jax
import jax.numpy as jnp
from jax import lax
from jax.experimental import pallas as pl
from jax.experimental.pallas import tpu as pltpu

D_MODEL = 1024
D_RNN = D_MODEL
D_POOL = D_MODEL
D_FF = 3 * D_MODEL
N_LRU_BLOCKS = 16
LRU_BLOCK = D_RNN // N_LRU_BLOCKS
LRU_CONV = 4
LRU_C = 8.0
POOL_WINDOWS = (2, 4, 8, 16)
POOL_GROUP = D_POOL // len(POOL_WINDOWS)
POOL_HIST = max(POOL_WINDOWS) - 1
FFN_CONV = 3
EPS = 1e-6
PAST_LEN = 2048

MXU_DIM_V7X = 256
GATE_CHUNKS = D_RNN // MXU_DIM_V7X
VMEM_LIMIT_BYTES = 58 * 1024 * 1024
ROW_CHUNK = 64
FFN_COL_CHUNK = 512
PROMPT_TT = 32
PROMPT_FFN_TT = 64
GELU_C1 = math.sqrt(2.0 / math.pi)
GELU_C2 = GELU_C1 * 0.044715

_BF16 = jnp.bfloat16
_F32 = jnp.float32


def _sigmoid(x):
    return 0.5 * jnp.tanh(0.5 * x) + 0.5


def _dot(a, b):
    return jnp.dot(a, b, preferred_element_type=_F32)


def _rms_scale(x):
    return x * lax.rsqrt(jnp.mean(x * x, axis=-1, keepdims=True) + EPS)


def _zero_after(pieces):
    acc = None
    for p in pieces:
        bits = pltpu.bitcast(p, jnp.uint32)
        z = lax.shift_right_logical(lax.shift_right_logical(bits, jnp.uint32(16)), jnp.uint32(16))
        acc = z if acc is None else acc | z
    return pltpu.bitcast(acc, _F32)[0:1, :]


def _mixer_kernel(
        x_first_ref, x_next_ref, h0_ref, lru0_ref, pool0_ref,
        norm_mix_ref, w_in_ref, conv_lru_w_ref, conv_lru_b_ref, w_gate_ref, b_ra_ref, b_ix_ref,
        lam_ref, w_pool_ref, pool_scale_ref, w_br_lru_ref, w_br_pool_ref, w_out_ref,
        x1_ref, h_ref, lru_new_ref, pool_new_ref,
        rnn_buf, pool_buf, gate_buf, tmp_buf, xn_buf, u_bf, pp_buf, res_pre, xn_pre,
        *, nb, tt, offset):
    tm = nb * tt
    lru_hist = (LRU_CONV - 1) * nb
    pool_hist = POOL_HIST * nb
    step = pl.program_id(0)
    rows = ROW_CHUNK
    row_chunks = [slice(r0, r0 + rows) for r0 in range(0, tm, rows)]
    g_mix = norm_mix_ref[...]

    def load_norm(x_ref):
        for t0 in range(0, tt, 8):
            blk = jnp.swapaxes(x_ref[:, t0:t0 + 8, :], 0, 1)
            res_pre[t0 * nb:(t0 + 8) * nb, :] = blk.reshape(8 * nb, D_MODEL)
        for sl in row_chunks:
            xn_pre[sl, :] = (_rms_scale(res_pre[sl, :]) * g_mix).astype(_BF16)

    @pl.when(step == 0)
    def _():
        h_ref[...] = h0_ref[...]
        rnn_buf[0:lru_hist, :] = lru0_ref[...]
        pool_buf[0:pool_hist, :] = pool0_ref[...]
        load_norm(x_first_ref)

    def tmp(k):
        return tmp_buf.at[:, k * D_MODEL:(k + 1) * D_MODEL]
    u_tmp, r_tmp, i_tmp, a_tmp, b_tmp, yb_tmp = (tmp(k) for k in range(6))
    ya_tmp, h_tmp = u_tmp, b_tmp

    x1_ref[...] = res_pre[...]
    xn_buf[...] = xn_pre[...]
    xn = xn_buf[...]
    rnn_buf[lru_hist:, :] = _dot(xn, w_in_ref[:, 0:D_RNN])

    load_norm(x_next_ref)
    prepared = _zero_after([xn_pre[sl.start:sl.start + 16, 0:128] for sl in row_chunks])
    conv_lru_b = conv_lru_b_ref[...] + jnp.concatenate([prepared] * (D_RNN // 128), axis=1)

    pool_buf[pool_hist:, :] = _dot(xn, w_in_ref[:, D_RNN:D_RNN + D_POOL])

    for sl in row_chunks:
        u = rnn_buf[sl, :] * conv_lru_w_ref[0:1, :]
        for k in range(1, LRU_CONV):
            u = u + rnn_buf[sl.start + k * nb:sl.stop + k * nb, :] * conv_lru_w_ref[k:k + 1, :]
        u = u + conv_lru_b
        u_tmp[sl, :] = u
        u_bf[sl, :] = u.astype(_BF16)

    for c in range(GATE_CHUNKS):
        cols = slice(c * MXU_DIM_V7X, (c + 1) * MXU_DIM_V7X)
        ri = _dot(u_bf[:, cols], w_gate_ref[c])
        r_tmp[:, cols] = ri[:, :MXU_DIM_V7X]
        i_tmp[:, cols] = ri[:, MXU_DIM_V7X:]

    gate_buf[...] = _dot(xn, w_in_ref[:, D_RNN + D_POOL:])

    row_t = lax.broadcasted_iota(jnp.int32, (tm, 128), 0) // nb
    pos1 = row_t + (offset + 1) + step * tt
    for g, w in enumerate(POOL_WINDOWS):
        cols = slice(g * POOL_GROUP, (g + 1) * POOL_GROUP)
        first = POOL_HIST + 1 - w
        s = pool_buf[first * nb:(first + w - 1) * nb + tm, cols]
        span = 1
        while span < w:
            s = s[span * nb:, :] + s[:s.shape[0] - span * nb, :]
            span *= 2
        cnt = jnp.minimum(pos1, w).astype(_F32)
        cnt = jnp.concatenate([cnt] * (POOL_GROUP // 128), axis=1)
        pooled = s / cnt - pool_buf[pool_hist:, cols]
        pp = _dot(pooled.astype(_BF16), w_pool_ref[g]) * pool_scale_ref[:, cols]
        pp_buf[:, cols] = pp.astype(_BF16)

    lam = lam_ref[...]
    softplus_neg_lam = jnp.maximum(-lam, 0.0) + jnp.log1p(jnp.exp(-jnp.abs(lam)))
    quarter_scale = (-0.25 * LRU_C) * softplus_neg_lam
    half_b_ra = 0.5 * b_ra_ref[...]
    half_b_ix = 0.5 * b_ix_ref[...]

    for sl in row_chunks:
        tr = jnp.tanh(r_tmp[sl, :] + half_b_ra)
        ti = jnp.tanh(i_tmp[sl, :] + half_b_ix)
        z = jnp.tanh(tr * quarter_scale + quarter_scale)
        q = 1.0 / (1.0 - z)
        a_tmp[sl, :] = (1.0 + z) * q
        b_tmp[sl, :] = (jnp.sqrt(-z) * q) * ((ti + 1.0) * u_tmp[sl, :])
    yb_tmp[...] = _dot(pp_buf[...], w_br_pool_ref[...])

    h = h_ref[...]
    for t in range(tt):
        sl = slice(t * nb, (t + 1) * nb)
        h = a_tmp[sl, :] * h + b_tmp[sl, :]
        h_tmp[sl, :] = h
    h_ref[...] = h

    ya_tmp[...] = _dot(h_tmp[...].astype(_BF16), w_br_lru_ref[...])

    for sl in row_chunks:
        m = (_sigmoid(gate_buf[sl, 0:D_MODEL]) * ya_tmp[sl, :]
             + _sigmoid(gate_buf[sl, D_MODEL:]) * yb_tmp[sl, :])
        xn_buf[sl, :] = m.astype(_BF16)
    x1_ref[...] = x1_ref[...] + _dot(xn_buf[...], w_out_ref[...])

    rnn_buf[0:lru_hist, :] = rnn_buf[tm:tm + lru_hist, :]
    pool_buf[0:pool_hist, :] = pool_buf[tm:tm + pool_hist, :]

    @pl.when(step == pl.num_programs(0) - 1)
    def _():
        lru_new_ref[...] = rnn_buf[0:lru_hist, :]
        pool_new_ref[...] = pool_buf[0:pool_hist, :]


def _ffn_kernel(
        x1_ref, ffn0_ref, norm_ffn_ref, w_up_ref, conv_ffn_w_ref, conv_ffn_b_ref, w_down_ref,
        norm_final_ref,
        y_ref, ffn_new_ref,
        hup_buf, xn_buf, act_buf, res_buf,
        *, nb, tt):
    tm = nb * tt
    ffn_hist = (FFN_CONV - 1) * nb
    step = pl.program_id(0)
    rows = ROW_CHUNK
    row_chunks = [slice(r0, r0 + rows) for r0 in range(0, tm, rows)]

    @pl.when(step == 0)
    def _():
        hup_buf[0:ffn_hist, :] = ffn0_ref[...]

    g_ffn = norm_ffn_ref[...]
    for sl in row_chunks:
        xn_buf[sl, :] = (_rms_scale(x1_ref[sl, :]) * g_ffn).astype(_BF16)
    xn = xn_buf[...]
    for c in range(3):
        cols = slice(c * 2 * D_MODEL, (c + 1) * 2 * D_MODEL)
        hup_buf[ffn_hist:, cols] = _dot(xn, w_up_ref[:, cols])

    def conv_cols(sl, cols, scale):
        y = hup_buf[sl, cols] * (scale * conv_ffn_w_ref[0:1, cols])
        for k in range(1, FFN_CONV):
            taps = hup_buf[sl.start + k * nb:sl.stop + k * nb, cols]
            y = y + taps * (scale * conv_ffn_w_ref[k:k + 1, cols])
        return y + scale * conv_ffn_b_ref[:, cols]

    for sl in row_chunks:
        for c in range(D_FF // FFN_COL_CHUNK):
            gcols = slice(c * FFN_COL_CHUNK, (c + 1) * FFN_COL_CHUNK)
            vcols = slice(D_FF + c * FFN_COL_CHUNK, D_FF + (c + 1) * FFN_COL_CHUNK)
            gt = conv_cols(sl, gcols, 1.0)
            half_v = conv_cols(sl, vcols, 0.5)
            th = jnp.tanh(gt * (GELU_C2 * (gt * gt) + GELU_C1))
            act_buf[sl, gcols] = ((gt * (1.0 + th)) * half_v).astype(_BF16)

    res_buf[...] = x1_ref[...] + _dot(act_buf[...], w_down_ref[...])

    g_final = norm_final_ref[...]
    for t0 in range(0, tt, 8):
        sl = slice(t0 * nb, (t0 + 8) * nb)
        blk = (_rms_scale(res_buf[sl, :]) * g_final).reshape(8, nb, D_MODEL)
        y_ref[:, t0:t0 + 8, :] = jnp.swapaxes(blk, 0, 1)

    hup_buf[0:ffn_hist, :] = hup_buf[tm:tm + ffn_hist, :]

    @pl.when(step == pl.num_programs(0) - 1)
    def _():
        ffn_new_ref[...] = hup_buf[0:ffn_hist, :]


def _full_spec(a):
    zeros = (0,) * a.ndim
    return pl.BlockSpec(a.shape, lambda i: zeros)


def _compiler_params():
    return pltpu.CompilerParams(
        dimension_semantics=("arbitrary",), vmem_limit_bytes=VMEM_LIMIT_BYTES)


def _run_mixer(x, h0, lru0, pool0, weights, *, tt, offset, name):
    nb, t_len, _ = x.shape
    tm = nb * tt
    n_tiles = t_len // tt
    assert t_len % tt == 0 and tt % 8 == 0 and tm % ROW_CHUNK == 0 and tt > POOL_HIST
    small = (h0, lru0, pool0)
    x_first = x[:, :tt, :]
    return pl.pallas_call(
        functools.partial(_mixer_kernel, nb=nb, tt=tt, offset=offset),
        grid=(n_tiles,),
        in_specs=[_full_spec(x_first),
                  pl.BlockSpec((nb, tt, D_MODEL),
                               lambda i: (0, jnp.minimum(i + 1, n_tiles - 1), 0))]
        + [_full_spec(a) for a in small + tuple(weights)],
        out_specs=(pl.BlockSpec((tm, D_MODEL), lambda i: (i, 0)),)
        + tuple(_full_spec(a) for a in small),
        out_shape=(jax.ShapeDtypeStruct((t_len * nb, D_MODEL), _F32),)
        + tuple(jax.ShapeDtypeStruct(a.shape, _F32) for a in small),
        scratch_shapes=[
            pltpu.VMEM(((LRU_CONV - 1) * nb + tm, D_RNN), _F32),
            pltpu.VMEM((POOL_HIST * nb + tm, D_POOL), _F32),
            pltpu.VMEM((tm, 2 * D_MODEL), _F32),
            pltpu.VMEM((tm, 6 * D_MODEL), _F32),
            pltpu.VMEM((tm, D_MODEL), _BF16),
            pltpu.VMEM((tm, D_RNN), _BF16),
            pltpu.VMEM((tm, D_POOL), _BF16),
            pltpu.VMEM((tm, D_MODEL), _F32),
            pltpu.VMEM((tm, D_MODEL), _BF16),
        ],
        compiler_params=_compiler_params(),
        name=name,
    )(x_first, x, *small, *weights)


def _run_ffn(x1, ffn0, weights, *, nb, tt, name):
    tm = nb * tt
    t_len = x1.shape[0] // nb
    return pl.pallas_call(
        functools.partial(_ffn_kernel, nb=nb, tt=tt),
        grid=(t_len // tt,),
        in_specs=[pl.BlockSpec((tm, D_MODEL), lambda i: (i, 0)), _full_spec(ffn0)]
        + [_full_spec(a) for a in weights],
        out_specs=(pl.BlockSpec((nb, tt, D_MODEL), lambda i: (0, i, 0)), _full_spec(ffn0)),
        out_shape=(jax.ShapeDtypeStruct((nb, t_len, D_MODEL), _F32),
                   jax.ShapeDtypeStruct(ffn0.shape, _F32)),
        scratch_shapes=[
            pltpu.VMEM(((FFN_CONV - 1) * nb + tm, 2 * D_FF), _F32),
            pltpu.VMEM((tm, D_MODEL), _BF16),
            pltpu.VMEM((tm, D_FF), _BF16),
            pltpu.VMEM((tm, D_MODEL), _F32),
        ],
        compiler_params=_compiler_params(),
        name=name,
    )(x1, ffn0, *weights)


def _block_diag_tiles(w):
    per = MXU_DIM_V7X // LRU_BLOCK
    w4 = w.reshape(GATE_CHUNKS, per, LRU_BLOCK, LRU_BLOCK)
    eye = jnp.eye(per, dtype=w.dtype)
    return jnp.einsum('ciab,ij->ciajb', w4, eye).reshape(GATE_CHUNKS, MXU_DIM_V7X, MXU_DIM_V7X)


def _to_time_major(a):
    return jnp.swapaxes(a, 0, 1).reshape(a.shape[0] * a.shape[1], a.shape[2])


def _from_time_major(a, nb):
    return jnp.swapaxes(a.reshape(a.shape[0] // nb, nb, a.shape[1]), 0, 1)


def kernel(x_prompt, x_sample, state_lru_h, state_lru_conv, state_pool, state_ffn_conv, norm_mix, w_in, conv_lru_w, conv_lru_b, w_ra, b_ra, w_ix, b_ix, lru_lambda, w_pool, pool_scale, w_br_lru, w_br_pool, w_out, norm_ffn, w_up, conv_ffn_w, conv_ffn_b, w_down, norm_final):
    assert norm_mix.shape[0] == 1, "one layer"
    row = lambda v: v.reshape(1, -1)
    w_gate = jnp.concatenate([_block_diag_tiles(w_ra[0]), _block_diag_tiles(w_ix[0])], axis=-1)
    mixer_w = (
        row(norm_mix[0]), w_in[0].astype(_BF16), conv_lru_w[0], row(conv_lru_b[0]),
        (0.5 * w_gate).astype(_BF16), row(b_ra[0]), row(b_ix[0]), row(lru_lambda[0]),
        w_pool[0].astype(_BF16), row(pool_scale[0]),
        w_br_lru[0].astype(_BF16), w_br_pool[0].astype(_BF16), w_out[0].astype(_BF16),
    )
    ffn_w = (
        row(norm_ffn[0]), w_up[0].astype(_BF16), conv_ffn_w[0], row(conv_ffn_b[0]),
        w_down[0].astype(_BF16), row(norm_final),
    )

    def run(x, h0, lru0, pool0, ffn0, offset, tt, ffn_tt, name):
        nb = x.shape[0]
        x1, h, lru, pool = _run_mixer(
            x, h0, _to_time_major(lru0), _to_time_major(pool0), mixer_w,
            tt=tt, offset=offset, name=name + "_mixer")
        y, ffn = _run_ffn(x1, _to_time_major(ffn0), ffn_w, nb=nb, tt=ffn_tt, name=name + "_ffn")
        return (y, h[None], _from_time_major(lru, nb)[None],
                _from_time_major(pool, nb)[None], _from_time_major(ffn, nb)[None])

    nbp = x_prompt.shape[0]
    dt = x_prompt.dtype
    y_p, h_p, lru_p, pool_p, ffn_p = run(
        x_prompt,
        jnp.zeros((nbp, D_RNN), _F32),
        jnp.zeros((nbp, LRU_CONV - 1, D_RNN), dt),
        jnp.zeros((nbp, POOL_HIST, D_POOL), dt),
        jnp.zeros((nbp, FFN_CONV - 1, 2 * D_FF), dt),
        0, PROMPT_TT, PROMPT_FFN_TT, "prompt")
    y_s, h_s, lru_s, pool_s, ffn_s = run(
        x_sample, state_lru_h[0], state_lru_conv[0], state_pool[0], state_ffn_conv[0],
        PAST_LEN, x_sample.shape[1], x_sample.shape[1], "sample")
    return (y_p, y_s, h_p, lru_p, pool_p, ffn_p, h_s, lru_s, pool_s, ffn_s)
```

```python
import functools
import math

import jax
import jax.numpy as jnp
from jax import lax
from jax.experimental import pallas as pl
from jax.experimental.pallas import tpu as pltpu

D_MODEL = 1024
D_RNN = D_MODEL
D_POOL = D_MODEL
D_FF = 3 * D_MODEL
N_LRU_BLOCKS = 16
LRU_BLOCK = D_RNN // N_LRU_BLOCKS
LRU_CONV = 4
LRU_C = 8.0
POOL_WINDOWS = (2, 4, 8, 16)
POOL_GROUP = D_POOL // len(POOL_WINDOWS)
POOL_HIST = max(POOL_WINDOWS) - 1
FFN_CONV = 3
EPS = 1e-6
PAST_LEN = 2048

MXU_DIM_V7X = 256
GATE_CHUNKS = D_RNN // MXU_DIM_V7X
VMEM_LIMIT_BYTES = 58 * 1024 * 1024
ROW_CHUNK = 64
FFN_COL_CHUNK = 512
PROMPT_TT = 64
PROMPT_FFN_TT = 64
GELU_C1 = math.sqrt(2.0 / math.pi)
GELU_C2 = GELU_C1 * 0.044715

_BF16 = jnp.bfloat16
_F32 = jnp.float32


def _sigmoid(x):
    return 0.5 * jnp.tanh(0.5 * x) + 0.5


def _dot(a, b):
    return jnp.dot(a, b, preferred_element_type=_F32)


def _rms_scale(x):
    return x * lax.rsqrt(jnp.mean(x * x, axis=-1, keepdims=True) + EPS)


def _zero_after(pieces):
    acc = None
    for p in pieces:
        bits = pltpu.bitcast(p, jnp.uint32)
        z = lax.shift_right_logical(lax.shift_right_logical(bits, jnp.uint32(16)), jnp.uint32(16))
        acc = z if acc is None else acc | z
    return pltpu.bitcast(acc, _F32)[0:1, :]


def _mixer_kernel(
        x_first_ref, x_next_ref, h0_ref, lru0_ref, pool0_ref,
        norm_mix_ref, w_in_ref, conv_lru_w_ref, conv_lru_b_ref, w_gate_ref, b_ra_ref, b_ix_ref,
        lam_ref, w_pool_ref, pool_scale_ref, w_br_lru_ref, w_br_pool_ref, w_out_ref,
        x1_ref, h_ref, lru_new_ref, pool_new_ref,
        rnn_buf, pool_buf, gate_buf, tmp_buf, xn_buf, u_bf, pp_buf, res_pre, xn_pre,
        *, nb, tt, offset):
    tm = nb * tt
    lru_hist = (LRU_CONV - 1) * nb
    pool_hist = POOL_HIST * nb
    step = pl.program_id(0)
    rows = ROW_CHUNK
    row_chunks = [slice(r0, r0 + rows) for r0 in range(0, tm, rows)]
    g_mix = norm_mix_ref[...]

    def load_norm(x_ref):
        for t0 in range(0, tt, 8):
            blk = jnp.swapaxes(x_ref[:, t0:t0 + 8, :], 0, 1)
            res_pre[t0 * nb:(t0 + 8) * nb, :] = blk.reshape(8 * nb, D_MODEL)
        for sl in row_chunks:
            xn_pre[sl, :] = (_rms_scale(res_pre[sl, :]) * g_mix).astype(_BF16)

    @pl.when(step == 0)
    def _():
        h_ref[...] = h0_ref[...]
        rnn_buf[0:lru_hist, :] = lru0_ref[...]
        pool_buf[0:pool_hist, :] = pool0_ref[...]
        load_norm(x_first_ref)

    def tmp(k):
        return tmp_buf.at[:, k * D_MODEL:(k + 1) * D_MODEL]
    u_tmp, r_tmp, i_tmp, a_tmp, b_tmp, yb_tmp = (tmp(k) for k in range(6))
    ya_tmp, h_tmp = u_tmp, b_tmp

    x1_ref[...] = res_pre[...]
    xn_buf[...] = xn_pre[...]
    xn = xn_buf[...]
    rnn_buf[lru_hist:, :] = _dot(xn, w_in_ref[:, 0:D_RNN])

    load_norm(x_next_ref)
    prepared = _zero_after([xn_pre[sl.start:sl.start + 16, 0:128] for sl in row_chunks])
    conv_lru_b = conv_lru_b_ref[...] + jnp.concatenate([prepared] * (D_RNN // 128), axis=1)

    pool_buf[pool_hist:, :] = _dot(xn, w_in_ref[:, D_RNN:D_RNN + D_POOL])

    for sl in row_chunks:
        u = rnn_buf[sl, :] * conv_lru_w_ref[0:1, :]
        for k in range(1, LRU_CONV):
            u = u + rnn_buf[sl.start + k * nb:sl.stop + k * nb, :] * conv_lru_w_ref[k:k + 1, :]
        u = u + conv_lru_b
        u_tmp[sl, :] = u
        u_bf[sl, :] = u.astype(_BF16)

    for c in range(GATE_CHUNKS):
        cols = slice(c * MXU_DIM_V7X, (c + 1) * MXU_DIM_V7X)
        ri = _dot(u_bf[:, cols], w_gate_ref[c])
        r_tmp[:, cols] = ri[:, :MXU_DIM_V7X]
        i_tmp[:, cols] = ri[:, MXU_DIM_V7X:]

    gate_buf[...] = _dot(xn, w_in_ref[:, D_RNN + D_POOL:])

    row_t = lax.broadcasted_iota(jnp.int32, (tm, 128), 0) // nb
    pos1 = row_t + (offset + 1) + step * tt
    for g, w in enumerate(POOL_WINDOWS):
        cols = slice(g * POOL_GROUP, (g + 1) * POOL_GROUP)
        first = POOL_HIST + 1 - w
        s = pool_buf[first * nb:(first + w - 1) * nb + tm, cols]
        span = 1
        while span < w:
            s = s[span * nb:, :] + s[:s.shape[0] - span * nb, :]
            span *= 2
        cnt = jnp.minimum(pos1, w).astype(_F32)
        cnt = jnp.concatenate([cnt] * (POOL_GROUP // 128), axis=1)
        pooled = s / cnt - pool_buf[pool_hist:, cols]
        pp = _dot(pooled.astype(_BF16), w_pool_ref[g]) * pool_scale_ref[:, cols]
        pp_buf[:, cols] = pp.astype(_BF16)

    lam = lam_ref[...]
    softplus_neg_lam = jnp.maximum(-lam, 0.0) + jnp.log1p(jnp.exp(-jnp.abs(lam)))
    quarter_scale = (-0.25 * LRU_C) * softplus_neg_lam
    half_b_ra = 0.5 * b_ra_ref[...]
    half_b_ix = 0.5 * b_ix_ref[...]

    for sl in row_chunks:
        tr = jnp.tanh(r_tmp[sl, :] + half_b_ra)
        ti = jnp.tanh(i_tmp[sl, :] + half_b_ix)
        z = jnp.tanh(tr * quarter_scale + quarter_scale)
        q = 1.0 / (1.0 - z)
        a_tmp[sl, :] = (1.0 + z) * q
        b_tmp[sl, :] = (jnp.sqrt(-z) * q) * ((ti + 1.0) * u_tmp[sl, :])
    yb_tmp[...] = _dot(pp_buf[...], w_br_pool_ref[...])

    h = h_ref[...]
    for t in range(tt):
        sl = slice(t * nb, (t + 1) * nb)
        h = a_tmp[sl, :] * h + b_tmp[sl, :]
        h_tmp[sl, :] = h
    h_ref[...] = h

    ya_tmp[...] = _dot(h_tmp[...].astype(_BF16), w_br_lru_ref[...])

    for sl in row_chunks:
        m = (_sigmoid(gate_buf[sl, 0:D_MODEL]) * ya_tmp[sl, :]
             + _sigmoid(gate_buf[sl, D_MODEL:]) * yb_tmp[sl, :])
        xn_buf[sl, :] = m.astype(_BF16)
    x1_ref[...] = x1_ref[...] + _dot(xn_buf[...], w_out_ref[...])

    rnn_buf[0:lru_hist, :] = rnn_buf[tm:tm + lru_hist, :]
    pool_buf[0:pool_hist, :] = pool_buf[tm:tm + pool_hist, :]

    @pl.when(step == pl.num_programs(0) - 1)
    def _():
        lru_new_ref[...] = rnn_buf[0:lru_hist, :]
        pool_new_ref[...] = pool_buf[0:pool_hist, :]


def _ffn_kernel(
        x1_ref, ffn0_ref, norm_ffn_ref, w_up_ref, conv_ffn_w_ref, conv_ffn_b_ref, w_down_ref,
        norm_final_ref,
        y_ref, ffn_new_ref,
        hup_buf, xn_buf, act_buf, res_buf,
        *, nb, tt):
    tm = nb * tt
    ffn_hist = (FFN_CONV - 1) * nb
    step = pl.program_id(0)
    rows = ROW_CHUNK
    row_chunks = [slice(r0, r0 + rows) for r0 in range(0, tm, rows)]

    @pl.when(step == 0)
    def _():
        hup_buf[0:ffn_hist, :] = ffn0_ref[...]

    g_ffn = norm_ffn_ref[...]
    for sl in row_chunks:
        xn_buf[sl, :] = (_rms_scale(x1_ref[sl, :]) * g_ffn).astype(_BF16)
    xn = xn_buf[...]
    for c in range(3):
        cols = slice(c * 2 * D_MODEL, (c + 1) * 2 * D_MODEL)
        hup_buf[ffn_hist:, cols] = _dot(xn, w_up_ref[:, cols])

    def conv_cols(sl, cols, scale):
        y = hup_buf[sl, cols] * (scale * conv_ffn_w_ref[0:1, cols])
        for k in range(1, FFN_CONV):
            taps = hup_buf[sl.start + k * nb:sl.stop + k * nb, cols]
            y = y + taps * (scale * conv_ffn_w_ref[k:k + 1, cols])
        return y + scale * conv_ffn_b_ref[:, cols]

    for sl in row_chunks:
        for c in range(D_FF // FFN_COL_CHUNK):
            gcols = slice(c * FFN_COL_CHUNK, (c + 1) * FFN_COL_CHUNK)
            vcols = slice(D_FF + c * FFN_COL_CHUNK, D_FF + (c + 1) * FFN_COL_CHUNK)
            gt = conv_cols(sl, gcols, 1.0)
            half_v = conv_cols(sl, vcols, 0.5)
            th = jnp.tanh(gt * (GELU_C2 * (gt * gt) + GELU_C1))
            act_buf[sl, gcols] = ((gt * (1.0 + th)) * half_v).astype(_BF16)

    res_buf[...] = x1_ref[...] + _dot(act_buf[...], w_down_ref[...])

    g_final = norm_final_ref[...]
    for t0 in range(0, tt, 8):
        sl = slice(t0 * nb, (t0 + 8) * nb)
        blk = (_rms_scale(res_buf[sl, :]) * g_final).reshape(8, nb, D_MODEL)
        y_ref[:, t0:t0 + 8, :] = jnp.swapaxes(blk, 0, 1)

    hup_buf[0:ffn_hist, :] = hup_buf[tm:tm + ffn_hist, :]

    @pl.when(step == pl.num_programs(0) - 1)
    def _():
        ffn_new_ref[...] = hup_buf[0:ffn_hist, :]


def _full_spec(a):
    zeros = (0,) * a.ndim
    return pl.BlockSpec(a.shape, lambda i: zeros)


def _compiler_params():
    return pltpu.CompilerParams(
        dimension_semantics=("arbitrary",), vmem_limit_bytes=VMEM_LIMIT_BYTES)


def _run_mixer(x, h0, lru0, pool0, weights, *, tt, offset, name):
    nb, t_len, _ = x.shape
    tm = nb * tt
    n_tiles = t_len // tt
    assert t_len % tt == 0 and tt % 8 == 0 and tm % ROW_CHUNK == 0 and tt > POOL_HIST
    small = (h0, lru0, pool0)
    x_first = x[:, :tt, :]
    return pl.pallas_call(
        functools.partial(_mixer_kernel, nb=nb, tt=tt, offset=offset),
        grid=(n_tiles,),
        in_specs=[_full_spec(x_first),
                  pl.BlockSpec((nb, tt, D_MODEL),
                               lambda i: (0, jnp.minimum(i + 1, n_tiles - 1), 0))]
        + [_full_spec(a) for a in small + tuple(weights)],
        out_specs=(pl.BlockSpec((tm, D_MODEL), lambda i: (i, 0)),)
        + tuple(_full_spec(a) for a in small),
        out_shape=(jax.ShapeDtypeStruct((t_len * nb, D_MODEL), _F32),)
        + tuple(jax.ShapeDtypeStruct(a.shape, _F32) for a in small),
        scratch_shapes=[
            pltpu.VMEM(((LRU_CONV - 1) * nb + tm, D_RNN), _F32),
            pltpu.VMEM((POOL_HIST * nb + tm, D_POOL), _F32),
            pltpu.VMEM((tm, 2 * D_MODEL), _F32),
            pltpu.VMEM((tm, 6 * D_MODEL), _F32),
            pltpu.VMEM((tm, D_MODEL), _BF16),
            pltpu.VMEM((tm, D_RNN), _BF16),
            pltpu.VMEM((tm, D_POOL), _BF16),
            pltpu.VMEM((tm, D_MODEL), _F32),
            pltpu.VMEM((tm, D_MODEL), _BF16),
        ],
        compiler_params=_compiler_params(),
        name=name,
    )(x_first, x, *small, *weights)


def _run_ffn(x1, ffn0, weights, *, nb, tt, name):
    tm = nb * tt
    t_len = x1.shape[0] // nb
    return pl.pallas_call(
        functools.partial(_ffn_kernel, nb=nb, tt=tt),
        grid=(t_len // tt,),
        in_specs=[pl.BlockSpec((tm, D_MODEL), lambda i: (i, 0)), _full_spec(ffn0)]
        + [_full_spec(a) for a in weights],
        out_specs=(pl.BlockSpec((nb, tt, D_MODEL), lambda i: (0, i, 0)), _full_spec(ffn0)),
        out_shape=(jax.ShapeDtypeStruct((nb, t_len, D_MODEL), _F32),
                   jax.ShapeDtypeStruct(ffn0.shape, _F32)),
        scratch_shapes=[
            pltpu.VMEM(((FFN_CONV - 1) * nb + tm, 2 * D_FF), _F32),
            pltpu.VMEM((tm, D_MODEL), _BF16),
            pltpu.VMEM((tm, D_FF), _BF16),
            pltpu.VMEM((tm, D_MODEL), _F32),
        ],
        compiler_params=_compiler_params(),
        name=name,
    )(x1, ffn0, *weights)


def _block_diag_tiles(w):
    per = MXU_DIM_V7X // LRU_BLOCK
    w4 = w.reshape(GATE_CHUNKS, per, LRU_BLOCK, LRU_BLOCK)
    eye = jnp.eye(per, dtype=w.dtype)
    return jnp.einsum('ciab,ij->ciajb', w4, eye).reshape(GATE_CHUNKS, MXU_DIM_V7X, MXU_DIM_V7X)


def _to_time_major(a):
    return jnp.swapaxes(a, 0, 1).reshape(a.shape[0] * a.shape[1], a.shape[2])


def _from_time_major(a, nb):
    return jnp.swapaxes(a.reshape(a.shape[0] // nb, nb, a.shape[1]), 0, 1)


def kernel(x_prompt, x_sample, state_lru_h, state_lru_conv, state_pool, state_ffn_conv, norm_mix, w_in, conv_lru_w, conv_lru_b, w_ra, b_ra, w_ix, b_ix, lru_lambda, w_pool, pool_scale, w_br_lru, w_br_pool, w_out, norm_ffn, w_up, conv_ffn_w, conv_ffn_b, w_down, norm_final):
    assert norm_mix.shape[0] == 1, "one layer"
    row = lambda v: v.reshape(1, -1)
    w_gate = jnp.concatenate([_block_diag_tiles(w_ra[0]), _block_diag_tiles(w_ix[0])], axis=-1)
    mixer_w = (
        row(norm_mix[0]), w_in[0].astype(_BF16), conv_lru_w[0], row(conv_lru_b[0]),
        (0.5 * w_gate).astype(_BF16), row(b_ra[0]), row(b_ix[0]), row(lru_lambda[0]),
        w_pool[0].astype(_BF16), row(pool_scale[0]),
        w_br_lru[0].astype(_BF16), w_br_pool[0].astype(_BF16), w_out[0].astype(_BF16),
    )
    ffn_w = (
        row(norm_ffn[0]), w_up[0].astype(_BF16), conv_ffn_w[0], row(conv_ffn_b[0]),
        w_down[0].astype(_BF16), row(norm_final),
    )

    def run(x, h0, lru0, pool0, ffn0, offset, tt, ffn_tt, name):
        nb = x.shape[0]
        x1, h, lru, pool = _run_mixer(
            x, h0, _to_time_major(lru0), _to_time_major(pool0), mixer_w,
            tt=tt, offset=offset, name=name + "_mixer")
        y, ffn = _run_ffn(x1, _to_time_major(ffn0), ffn_w, nb=nb, tt=ffn_tt, name=name + "_ffn")
        return (y, h[None], _from_time_major(lru, nb)[None],
                _from_time_major(pool, nb)[None], _from_time_major(ffn, nb)[None])

    nbp = x_prompt.shape[0]
    dt = x_prompt.dtype
    y_p, h_p, lru_p, pool_p, ffn_p = run(
        x_prompt,
        jnp.zeros((nbp, D_RNN), _F32),
        jnp.zeros((nbp, LRU_CONV - 1, D_RNN), dt),
        jnp.zeros((nbp, POOL_HIST, D_POOL), dt),
        jnp.zeros((nbp, FFN_CONV - 1, 2 * D_FF), dt),
        0, PROMPT_TT, PROMPT_FFN_TT, "prompt")
    y_s, h_s, lru_s, pool_s, ffn_s = run(
        x_sample, state_lru_h[0], state_lru_conv[0], state_pool[0], state_ffn_conv[0],
        PAST_LEN, x_sample.shape[1], x_sample.shape[1], "sample")
    return (y_p, y_s, h_p, lru_p, pool_p, ffn_p, h_s, lru_s, pool_s, ffn_s)
```

```python
import functools
import math

import jax
import jax.numpy as jnp
from jax import lax
from jax.experimental import pallas as pl
from jax.experimental.pallas import tpu as pltpu

D_MODEL = 1024
D_RNN = D_MODEL
D_POOL = D_MODEL
D_FF = 3 * D_MODEL
N_LRU_BLOCKS = 16
LRU_BLOCK = D_RNN // N_LRU_BLOCKS
LRU_CONV = 4
LRU_C = 8.0
POOL_WINDOWS = (2, 4, 8, 16)
POOL_GROUP = D_POOL // len(POOL_WINDOWS)
POOL_HIST = max(POOL_WINDOWS) - 1
FFN_CONV = 3
EPS = 1e-6
PAST_LEN = 2048

BF16_SUBLANES = 16
MXU_DIM_V7X = 256
GATE_CHUNKS = D_RNN // MXU_DIM_V7X
VMEM_LIMIT_BYTES = 58 * 1024 * 1024
ROW_CHUNK = 64
FFN_COL_CHUNK = 512
PROMPT_TT = 64
PROMPT_FFN_TT = 64
GELU_C1 = math.sqrt(2.0 / math.pi)
GELU_C2 = GELU_C1 * 0.044715

_BF16 = jnp.bfloat16
_F32 = jnp.float32


def _sigmoid(x):
    return 0.5 * jnp.tanh(0.5 * x) + 0.5


def _dot(a, b):
    return jnp.dot(a, b, preferred_element_type=_F32)


def _rms_scale(x):
    return x * lax.rsqrt(jnp.mean(x * x, axis=-1, keepdims=True) + EPS)


def _zero_after(pieces):
    acc = None
    for p in pieces:
        bits = pltpu.bitcast(p, jnp.uint32)
        z = lax.shift_right_logical(lax.shift_right_logical(bits, jnp.uint32(16)), jnp.uint32(16))
        acc = z if acc is None else acc | z
    return pltpu.bitcast(acc, _F32)[0:1, :]


def _mixer_kernel(
        x_first_ref, x_next_ref, h0_ref, lru0_ref, pool0_ref,
        norm_mix_ref, w_in_ref, conv_lru_w_ref, conv_lru_b_ref, w_gate_ref, b_ra_ref, b_ix_ref,
        lam_ref, w_pool_ref, pool_scale_ref, w_br_lru_ref, w_br_pool_ref, w_out_ref,
        x1_ref, h_ref, lru_new_ref, pool_new_ref,
        rnn_buf, pool_buf, gate_buf, tmp_buf, xn_buf, u_bf, pp_buf, res_pre, xn_pre,
        *, nb, tt, offset):
    tm = nb * tt
    lru_hist = (LRU_CONV - 1) * nb
    pool_hist = POOL_HIST * nb
    step = pl.program_id(0)
    rows = ROW_CHUNK
    row_chunks = [slice(r0, r0 + rows) for r0 in range(0, tm, rows)]
    g_mix = norm_mix_ref[...]

    def load_norm(x_ref):
        for t0 in range(0, tt, 8):
            blk = jnp.swapaxes(x_ref[:, t0:t0 + 8, :], 0, 1)
            res_pre[t0 * nb:(t0 + 8) * nb, :] = blk.reshape(8 * nb, D_MODEL)
        for sl in row_chunks:
            xn_pre[sl, :] = (_rms_scale(res_pre[sl, :]) * g_mix).astype(_BF16)

    @pl.when(step == 0)
    def _():
        h_ref[...] = h0_ref[...]
        rnn_buf[0:lru_hist, :] = lru0_ref[...]
        pool_buf[0:pool_hist, :] = pool0_ref[...]
        load_norm(x_first_ref)

    def tmp(k):
        return tmp_buf.at[:, k * D_MODEL:(k + 1) * D_MODEL]
    u_tmp, r_tmp, i_tmp, a_tmp, b_tmp, yb_tmp = (tmp(k) for k in range(6))
    ya_tmp, h_tmp = u_tmp, b_tmp

    x1_ref[...] = res_pre[...]
    xn_buf[...] = xn_pre[...]
    xn = xn_buf[...]
    rnn_buf[lru_hist:, :] = _dot(xn, w_in_ref[:, 0:D_RNN])

    load_norm(x_next_ref)
    prepared = _zero_after([xn_pre[sl.start:sl.start + 16, 0:128] for sl in row_chunks])
    conv_lru_b = conv_lru_b_ref[...] + jnp.concatenate([prepared] * (D_RNN // 128), axis=1)

    pool_buf[pool_hist:, :] = _dot(xn, w_in_ref[:, D_RNN:D_RNN + D_POOL])

    for sl in row_chunks:
        u = rnn_buf[sl, :] * conv_lru_w_ref[0:1, :]
        for k in range(1, LRU_CONV):
            u = u + rnn_buf[sl.start + k * nb:sl.stop + k * nb, :] * conv_lru_w_ref[k:k + 1, :]
        u = u + conv_lru_b
        u_tmp[sl, :] = u
        u_bf[sl, :] = u.astype(_BF16)

    for c in range(GATE_CHUNKS):
        cols = slice(c * MXU_DIM_V7X, (c + 1) * MXU_DIM_V7X)
        ri = _dot(u_bf[:, cols], w_gate_ref[c])
        r_tmp[:, cols] = ri[:, :MXU_DIM_V7X]
        i_tmp[:, cols] = ri[:, MXU_DIM_V7X:]

    gate_buf[...] = _dot(xn, w_in_ref[:, D_RNN + D_POOL:])

    row_t = lax.broadcasted_iota(jnp.int32, (tm, 128), 0) // nb
    pos1 = row_t + (offset + 1) + step * tt
    for g, w in enumerate(POOL_WINDOWS):
        cols = slice(g * POOL_GROUP, (g + 1) * POOL_GROUP)
        first = POOL_HIST + 1 - w
        s = pool_buf[first * nb:(first + w - 1) * nb + tm, cols]
        span = 1
        while span < w:
            s = s[span * nb:, :] + s[:s.shape[0] - span * nb, :]
            span *= 2
        cnt = jnp.minimum(pos1, w).astype(_F32)
        cnt = jnp.concatenate([cnt] * (POOL_GROUP // 128), axis=1)
        pooled = s / cnt - pool_buf[pool_hist:, cols]
        pp = _dot(pooled.astype(_BF16), w_pool_ref[g]) * pool_scale_ref[:, cols]
        pp_buf[:, cols] = pp.astype(_BF16)

    lam = lam_ref[...]
    softplus_neg_lam = jnp.maximum(-lam, 0.0) + jnp.log1p(jnp.exp(-jnp.abs(lam)))
    quarter_scale = (-0.25 * LRU_C) * softplus_neg_lam
    half_b_ra = 0.5 * b_ra_ref[...]
    half_b_ix = 0.5 * b_ix_ref[...]

    for sl in row_chunks:
        tr = jnp.tanh(r_tmp[sl, :] + half_b_ra)
        ti = jnp.tanh(i_tmp[sl, :] + half_b_ix)
        z = jnp.tanh(tr * quarter_scale + quarter_scale)
        q = 1.0 / (1.0 - z)
        a_tmp[sl, :] = (1.0 + z) * q
        b_tmp[sl, :] = (jnp.sqrt(-z) * q) * ((ti + 1.0) * u_tmp[sl, :])
    yb_tmp[...] = _dot(pp_buf[...], w_br_pool_ref[...])

    h = h_ref[...]
    for t in range(tt):
        sl = slice(t * nb, (t + 1) * nb)
        h = a_tmp[sl, :] * h + b_tmp[sl, :]
        h_tmp[sl, :] = h
    h_ref[...] = h

    ya_tmp[...] = _dot(h_tmp[...].astype(_BF16), w_br_lru_ref[...])

    for sl in row_chunks:
        m = (_sigmoid(gate_buf[sl, 0:D_MODEL]) * ya_tmp[sl, :]
             + _sigmoid(gate_buf[sl, D_MODEL:]) * yb_tmp[sl, :])
        xn_buf[sl, :] = m.astype(_BF16)
    x1_ref[...] = x1_ref[...] + _dot(xn_buf[...], w_out_ref[...])

    rnn_buf[0:lru_hist, :] = rnn_buf[tm:tm + lru_hist, :]
    pool_buf[0:pool_hist, :] = pool_buf[tm:tm + pool_hist, :]

    @pl.when(step == pl.num_programs(0) - 1)
    def _():
        lru_new_ref[...] = rnn_buf[0:lru_hist, :]
        pool_new_ref[...] = pool_buf[0:pool_hist, :]


def _mixer_and_cast_kernel(*refs, n_in, n_out, n_cast, **kw):
    ins, cast_in = refs[:n_in], refs[n_in:n_in + n_cast]
    outs = refs[n_in + n_cast:n_in + n_cast + n_out]
    cast_out = refs[n_in + n_cast + n_out:n_in + 2 * n_cast + n_out]
    scratch = refs[n_in + 2 * n_cast + n_out:]
    for src, dst in zip(cast_in, cast_out):
        dst[...] = src[...].astype(dst.dtype)
    _mixer_kernel(*ins, *outs, *scratch, **kw)


def _ffn_kernel(
        x1_ref, ffn0_ref, norm_ffn_ref, w_up_ref, conv_ffn_w_ref, conv_ffn_b_ref, w_down_ref,
        norm_final_ref,
        y_ref, ffn_new_ref,
        hup_buf, xn_buf, act_buf, res_buf,
        *, nb, tt):
    tm = nb * tt
    ffn_hist = (FFN_CONV - 1) * nb
    step = pl.program_id(0)
    rows = ROW_CHUNK
    row_chunks = [slice(r0, r0 + rows) for r0 in range(0, tm, rows)]

    @pl.when(step == 0)
    def _():
        hup_buf[0:ffn_hist, :] = ffn0_ref[...]

    g_ffn = norm_ffn_ref[...]
    for sl in row_chunks:
        xn_buf[sl, :] = (_rms_scale(x1_ref[sl, :]) * g_ffn).astype(_BF16)
    xn = xn_buf[...]
    for c in range(3):
        cols = slice(c * 2 * D_MODEL, (c + 1) * 2 * D_MODEL)
        hup_buf[ffn_hist:, cols] = _dot(xn, w_up_ref[:, cols])

    def conv_cols(sl, cols, scale):
        y = hup_buf[sl, cols] * (scale * conv_ffn_w_ref[0:1, cols])
        for k in range(1, FFN_CONV):
            taps = hup_buf[sl.start + k * nb:sl.stop + k * nb, cols]
            y = y + taps * (scale * conv_ffn_w_ref[k:k + 1, cols])
        return y + scale * conv_ffn_b_ref[:, cols]

    for sl in row_chunks:
        for c in range(D_FF // FFN_COL_CHUNK):
            gcols = slice(c * FFN_COL_CHUNK, (c + 1) * FFN_COL_CHUNK)
            vcols = slice(D_FF + c * FFN_COL_CHUNK, D_FF + (c + 1) * FFN_COL_CHUNK)
            gt = conv_cols(sl, gcols, 1.0)
            half_v = conv_cols(sl, vcols, 0.5)
            th = jnp.tanh(gt * (GELU_C2 * (gt * gt) + GELU_C1))
            act_buf[sl, gcols] = ((gt * (1.0 + th)) * half_v).astype(_BF16)

    res_buf[...] = x1_ref[...] + _dot(act_buf[...], w_down_ref[...])

    g_final = norm_final_ref[...]
    for t0 in range(0, tt, 8):
        sl = slice(t0 * nb, (t0 + 8) * nb)
        blk = (_rms_scale(res_buf[sl, :]) * g_final).reshape(8, nb, D_MODEL)
        y_ref[:, t0:t0 + 8, :] = jnp.swapaxes(blk, 0, 1)

    hup_buf[0:ffn_hist, :] = hup_buf[tm:tm + ffn_hist, :]

    @pl.when(step == pl.num_programs(0) - 1)
    def _():
        ffn_new_ref[...] = hup_buf[0:ffn_hist, :]


def _full_spec(a):
    zeros = (0,) * a.ndim
    return pl.BlockSpec(a.shape, lambda i: zeros)


def _compiler_params():
    return pltpu.CompilerParams(
        dimension_semantics=("arbitrary",), vmem_limit_bytes=VMEM_LIMIT_BYTES)


def _run_mixer(x, h0, lru0, pool0, weights, *, tt, offset, name, cast=()):
    nb, t_len, _ = x.shape
    tm = nb * tt
    n_tiles = t_len // tt
    assert t_len % tt == 0 and tt % 8 == 0 and tm % ROW_CHUNK == 0 and tt > POOL_HIST
    small = (h0, lru0, pool0)
    x_first = x[:, :tt, :]
    inputs = (x_first, x) + small + tuple(weights)

    def slab_spec(a):
        assert a.shape[0] % (BF16_SUBLANES * n_tiles) == 0
        return pl.BlockSpec((a.shape[0] // n_tiles, a.shape[1]), lambda i: (i, 0))

    return pl.pallas_call(
        functools.partial(_mixer_and_cast_kernel, n_in=len(inputs), n_out=1 + len(small),
                          n_cast=len(cast), nb=nb, tt=tt, offset=offset),
        grid=(n_tiles,),
        in_specs=[_full_spec(x_first),
                  pl.BlockSpec((nb, tt, D_MODEL),
                               lambda i: (0, jnp.minimum(i + 1, n_tiles - 1), 0))]
        + [_full_spec(a) for a in small + tuple(weights)] + [slab_spec(a) for a in cast],
        out_specs=(pl.BlockSpec((tm, D_MODEL), lambda i: (i, 0)),)
        + tuple(_full_spec(a) for a in small) + tuple(slab_spec(a) for a in cast),
        out_shape=(jax.ShapeDtypeStruct((t_len * nb, D_MODEL), _F32),)
        + tuple(jax.ShapeDtypeStruct(a.shape, _F32) for a in small)
        + tuple(jax.ShapeDtypeStruct(a.shape, _BF16) for a in cast),
        scratch_shapes=[
            pltpu.VMEM(((LRU_CONV - 1) * nb + tm, D_RNN), _F32),
            pltpu.VMEM((POOL_HIST * nb + tm, D_POOL), _F32),
            pltpu.VMEM((tm, 2 * D_MODEL), _F32),
            pltpu.VMEM((tm, 6 * D_MODEL), _F32),
            pltpu.VMEM((tm, D_MODEL), _BF16),
            pltpu.VMEM((tm, D_RNN), _BF16),
            pltpu.VMEM((tm, D_POOL), _BF16),
            pltpu.VMEM((tm, D_MODEL), _F32),
            pltpu.VMEM((tm, D_MODEL), _BF16),
        ],
        compiler_params=_compiler_params(),
        name=name,
    )(*inputs, *cast)


def _run_ffn(x1, ffn0, weights, *, nb, tt, name):
    tm = nb * tt
    t_len = x1.shape[0] // nb
    return pl.pallas_call(
        functools.partial(_ffn_kernel, nb=nb, tt=tt),
        grid=(t_len // tt,),
        in_specs=[pl.BlockSpec((tm, D_MODEL), lambda i: (i, 0)), _full_spec(ffn0)]
        + [_full_spec(a) for a in weights],
        out_specs=(pl.BlockSpec((nb, tt, D_MODEL), lambda i: (0, i, 0)), _full_spec(ffn0)),
        out_shape=(jax.ShapeDtypeStruct((nb, t_len, D_MODEL), _F32),
                   jax.ShapeDtypeStruct(ffn0.shape, _F32)),
        scratch_shapes=[
            pltpu.VMEM(((FFN_CONV - 1) * nb + tm, 2 * D_FF), _F32),
            pltpu.VMEM((tm, D_MODEL), _BF16),
            pltpu.VMEM((tm, D_FF), _BF16),
            pltpu.VMEM((tm, D_MODEL), _F32),
        ],
        compiler_params=_compiler_params(),
        name=name,
    )(x1, ffn0, *weights)


def _block_diag_tiles(w):
    per = MXU_DIM_V7X // LRU_BLOCK
    w4 = w.reshape(GATE_CHUNKS, per, LRU_BLOCK, LRU_BLOCK)
    eye = jnp.eye(per, dtype=w.dtype)
    return jnp.einsum('ciab,ij->ciajb', w4, eye).reshape(GATE_CHUNKS, MXU_DIM_V7X, MXU_DIM_V7X)


def _to_time_major(a):
    return jnp.swapaxes(a, 0, 1).reshape(a.shape[0] * a.shape[1], a.shape[2])


def _from_time_major(a, nb):
    return jnp.swapaxes(a.reshape(a.shape[0] // nb, nb, a.shape[1]), 0, 1)


def kernel(x_prompt, x_sample, state_lru_h, state_lru_conv, state_pool, state_ffn_conv, norm_mix, w_in, conv_lru_w, conv_lru_b, w_ra, b_ra, w_ix, b_ix, lru_lambda, w_pool, pool_scale, w_br_lru, w_br_pool, w_out, norm_ffn, w_up, conv_ffn_w, conv_ffn_b, w_down, norm_final):
    assert norm_mix.shape[0] == 1, "one layer"
    row = lambda v: v.reshape(1, -1)
    w_gate = jnp.concatenate([_block_diag_tiles(w_ra[0]), _block_diag_tiles(w_ix[0])], axis=-1)
    mixer_w = (
        row(norm_mix[0]), w_in[0].astype(_BF16), conv_lru_w[0], row(conv_lru_b[0]),
        (0.5 * w_gate).astype(_BF16), row(b_ra[0]), row(b_ix[0]), row(lru_lambda[0]),
        w_pool[0].astype(_BF16), row(pool_scale[0]),
        w_br_lru[0].astype(_BF16), w_br_pool[0].astype(_BF16), w_out[0].astype(_BF16),
    )
    def ffn_weights(w_up_bf, w_down_bf):
        return (row(norm_ffn[0]), w_up_bf, conv_ffn_w[0], row(conv_ffn_b[0]), w_down_bf,
                row(norm_final))

    def mixer(x, h0, lru0, pool0, offset, tt, name, cast=()):
        return _run_mixer(x, h0, _to_time_major(lru0), _to_time_major(pool0), mixer_w,
                          tt=tt, offset=offset, name=name + "_mixer", cast=cast)

    def ffn(x1, ffn0, ffn_w, nb, tt, name):
        return _run_ffn(x1, _to_time_major(ffn0), ffn_w, nb=nb, tt=tt, name=name + "_ffn")

    def states(nb, h, lru, pool, ffn_state):
        return (h[None], _from_time_major(lru, nb)[None], _from_time_major(pool, nb)[None],
                _from_time_major(ffn_state, nb)[None])

    nbp = x_prompt.shape[0]
    dt = x_prompt.dtype
    x1_p, h_p, lru_p, pool_p, w_up_bf, w_down_bf = mixer(
        x_prompt, jnp.zeros((nbp, D_RNN), _F32), jnp.zeros((nbp, LRU_CONV - 1, D_RNN), dt),
        jnp.zeros((nbp, POOL_HIST, D_POOL), dt), 0, PROMPT_TT, "prompt",
        cast=(w_up[0], w_down[0]))
    ffn_w = ffn_weights(w_up_bf, w_down_bf)
    y_p, ffn_p = ffn(x1_p, jnp.zeros((nbp, FFN_CONV - 1, 2 * D_FF), dt), ffn_w, nbp,
                     PROMPT_FFN_TT, "prompt")

    nbs, tts = x_sample.shape[0], x_sample.shape[1]
    x1_s, h_s, lru_s, pool_s = mixer(
        x_sample, state_lru_h[0], state_lru_conv[0], state_pool[0], PAST_LEN, tts, "sample")
    y_s, ffn_s = ffn(x1_s, state_ffn_conv[0], ffn_w, nbs, tts, "sample")

    return ((y_p, y_s) + states(nbp, h_p, lru_p, pool_p, ffn_p)
            + states(nbs, h_s, lru_s, pool_s, ffn_s))
```
